```python
import jax, jax.numpy as jnp
from jax import lax
import numpy as np

D_MODEL = 1024
BATCH = 16
SEQ = 4096
DEPTH = 1
DEC_BATCH = 16
DEC_SEQ = 2048
PAST_LEN = 128

POOL_WIDTH = D_MODEL // 2
POOL_GROUPS = 4
POOL_GROUP_DIM = POOL_WIDTH // POOL_GROUPS
POOL_WINDOWS = (2, 4, 8, 16)
ATTN_HEADS = 4
QK_HEAD_DIM = 64
V_HEAD_DIM = 2 * QK_HEAD_DIM
ATTN_WIDTH = ATTN_HEADS * V_HEAD_DIM
MIX_WIDTH = POOL_WIDTH + ATTN_WIDTH
QK_WIDTH = ATTN_HEADS * 2 * QK_HEAD_DIM
IN_WIDTH = POOL_WIDTH + 2 * QK_WIDTH + ATTN_WIDTH
ROPE_THETA = 10000.0
Q_BLOCK = 128
MOE_GROUPS = 4
EXPERTS_PER_GROUP = 8
N_EXPERTS = MOE_GROUPS * EXPERTS_PER_GROUP
TOP_K = 2
D_FF_EXPERT = 512
ROW_BLOCK = 128
EPS = 1e-6

kernel_name = "hymba_pool_diffattn_hmoe_encoder"


def rmsnorm(x, g):
    xf = x.astype(jnp.float32)
    y = xf * lax.rsqrt(jnp.mean(xf * xf, axis=-1, keepdims=True) + EPS)
    return (y * g.astype(jnp.float32)).astype(x.dtype)


def rope_tables(n):
    inv_freq = ROPE_THETA ** (-jnp.arange(0, QK_HEAD_DIM, 2, dtype=jnp.float32) / QK_HEAD_DIM)
    ang = jnp.arange(n, dtype=jnp.float32)[:, None] * inv_freq[None, :]
    return jnp.cos(ang), jnp.sin(ang)


def apply_rope(x, cos, sin):
    xf = x.astype(jnp.float32)
    c = cos[None, :, None, None, :]
    s = sin[None, :, None, None, :]
    x1, x2 = jnp.split(xf, 2, axis=-1)
    return jnp.concatenate([x1 * c - x2 * s, x2 * c + x1 * s], axis=-1).astype(x.dtype)


def pool_mixer(z, w_pool, pool_scale):
    B, N, _ = z.shape
    zg = z.reshape(B, N, POOL_GROUPS, POOL_GROUP_DIM).astype(jnp.float32)
    cs = jnp.concatenate([jnp.zeros((B, 1, POOL_GROUPS, POOL_GROUP_DIM), jnp.float32),
                          jnp.cumsum(zg, axis=1)], axis=1)
    t = jnp.arange(N)[:, None]
    half = jnp.array(POOL_WINDOWS, dtype=jnp.int32)[None, :] // 2
    lo = jnp.clip(t - half, 0, N)
    hi = jnp.clip(t + half, 0, N)
    gi = jnp.arange(POOL_GROUPS)
    win_sum = cs[:, hi, gi] - cs[:, lo, gi]
    cnt = (hi - lo).astype(jnp.float32)[None, :, :, None]
    pooled = (win_sum / cnt - zg).astype(z.dtype)
    out = jnp.einsum('bngc,gce->bnge', pooled, w_pool)
    return out.reshape(B, N, POOL_WIDTH) * pool_scale


def diff_attention(q, k, v, lam, subln_g, lam_init):
    B, N = q.shape[0], q.shape[1]
    nb = N // Q_BLOCK
    qb = q.reshape(B, nb, Q_BLOCK, ATTN_HEADS, 2, QK_HEAD_DIM).transpose(1, 0, 2, 3, 4, 5)
    scale = QK_HEAD_DIM ** -0.5

    def block(qblk):
        s = jnp.einsum('bqhcd,bkhcd->bhcqk', qblk, k, preferred_element_type=jnp.float32) * scale
        p = jax.nn.softmax(s, axis=-1)
        w = p[:, :, 0] - lam * p[:, :, 1]
        o = jnp.einsum('bhqk,bkhe->bqhe', w.astype(v.dtype), v, preferred_element_type=jnp.float32)
        return o.astype(v.dtype)

    o = lax.map(block, qb)
    o = o.transpose(1, 0, 2, 3, 4).reshape(B, N, ATTN_HEADS, V_HEAD_DIM)
    o = rmsnorm(o, subln_g) * (1.0 - lam_init)
    return o.reshape(B, N, ATTN_WIDTH)


def hier_moe(h, w_rg, b_rg, w_re, b_re, w_gate, w_up, w_down):
    B, N, D = h.shape
    T = B * N
    xt = h.reshape(T, D)
    pg = jax.nn.softmax((xt @ w_rg).astype(jnp.float32) + b_rg.astype(jnp.float32), axis=-1)
    g_top, g_idx = lax.top_k(pg, 1)
    le = ((xt @ w_re).astype(jnp.float32) + b_re.astype(jnp.float32)).reshape(T, MOE_GROUPS, EXPERTS_PER_GROUP)
    le_sel = le[jnp.arange(T), g_idx[:, 0]]
    pe = jax.nn.softmax(le_sel, axis=-1)
    e_top, e_loc = lax.top_k(pe, TOP_K)
    gates = g_top * (e_top / jnp.sum(e_top, axis=-1, keepdims=True))
    eid = g_idx * EXPERTS_PER_GROUP + e_loc

    A = T * TOP_K
    flat_e = eid.reshape(A)
    flat_g = gates.reshape(A)
    flat_t = jnp.repeat(jnp.arange(T, dtype=jnp.int32), TOP_K)
    order = jnp.argsort(flat_e)
    se = flat_e[order]
    counts = jnp.bincount(flat_e, length=N_EXPERTS)
    padded = (counts + ROW_BLOCK - 1) // ROW_BLOCK * ROW_BLOCK
    start = jnp.cumsum(counts) - counts
    pad_end = jnp.cumsum(padded)
    pad_start = pad_end - padded
    dest = pad_start[se] + (jnp.arange(A) - start[se])
    P = -(-A // ROW_BLOCK) * ROW_BLOCK + N_EXPERTS * ROW_BLOCK
    n_blk = P // ROW_BLOCK
    row_tok = jnp.full((P,), T, jnp.int32).at[dest].set(flat_t[order])
    row_gate = jnp.zeros((P,), jnp.float32).at[dest].set(flat_g[order])
    blk_e = jnp.minimum(jnp.searchsorted(pad_end, jnp.arange(n_blk) * ROW_BLOCK, side='right'),
                        N_EXPERTS - 1)
    xpad = jnp.concatenate([xt, jnp.zeros((1, D), xt.dtype)], axis=0)
    xb = xpad[row_tok].reshape(n_blk, ROW_BLOCK, D)

    def expert_block(args):
        xblk, e = args
        hid = jax.nn.silu(xblk @ w_gate[e]) * (xblk @ w_up[e])
        return hid @ w_down[e]

    yb = lax.map(expert_block, (xb, blk_e)).reshape(P, D)
    y = jax.ops.segment_sum(yb * row_gate[:, None].astype(yb.dtype), row_tok, num_segments=T + 1)[:T]
    return y.reshape(B, N, D).astype(h.dtype)


def encoder_layer(x, cos, sin, layer, lam_init, g_mix, w_in, w_pool, pool_scale,
                  lambda_q1, lambda_k1, lambda_q2, lambda_k2, subln_g, w_out, g_ffn,
                  w_router_group, b_router_group, w_router_expert, b_router_expert,
                  w_gate, w_up, w_down):
    B, N, _ = x.shape
    l = layer
    xn = rmsnorm(x, g_mix[l])
    z = xn @ w_in[l]
    z_pool = z[..., :POOL_WIDTH]
    q = z[..., POOL_WIDTH:POOL_WIDTH + QK_WIDTH].reshape(B, N, ATTN_HEADS, 2, QK_HEAD_DIM)
    k = z[..., POOL_WIDTH + QK_WIDTH:POOL_WIDTH + 2 * QK_WIDTH].reshape(B, N, ATTN_HEADS, 2, QK_HEAD_DIM)
    v = z[..., POOL_WIDTH + 2 * QK_WIDTH:].reshape(B, N, ATTN_HEADS, V_HEAD_DIM)
    pool_out = pool_mixer(z_pool, w_pool[l], pool_scale[l])
    q = apply_rope(q, cos, sin)
    k = apply_rope(k, cos, sin)
    f32 = jnp.float32
    lam = (jnp.exp(jnp.sum(lambda_q1[l].astype(f32) * lambda_k1[l].astype(f32)))
           - jnp.exp(jnp.sum(lambda_q2[l].astype(f32) * lambda_k2[l].astype(f32))) + lam_init)
    attn_out = diff_attention(q, k, v, lam, subln_g[l], lam_init)
    h = x + jnp.concatenate([pool_out, attn_out], axis=-1) @ w_out[l]
    ff = hier_moe(rmsnorm(h, g_ffn[l]), w_router_group[l], b_router_group[l],
                  w_router_expert[l], b_router_expert[l], w_gate[l], w_up[l], w_down[l])
    return h + ff


def run_trunk(x, g_mix, w_in, w_pool, pool_scale, lambda_q1, lambda_k1, lambda_q2, lambda_k2,
              subln_g, w_out, g_ffn, w_router_group, b_router_group, w_router_expert,
              b_router_expert, w_gate, w_up, w_down, g_final):
    cos, sin = rope_tables(x.shape[1])
    for layer in range(DEPTH):
        lam_init = 0.8 - 0.6 * float(np.exp(-0.3 * layer))
        x = encoder_layer(x, cos, sin, layer, lam_init, g_mix, w_in, w_pool, pool_scale,
                          lambda_q1, lambda_k1, lambda_q2, lambda_k2, subln_g, w_out, g_ffn,
                          w_router_group, b_router_group, w_router_expert, b_router_expert,
                          w_gate, w_up, w_down)
    return rmsnorm(x, g_final)


def setup_inputs(seed: int = 0) -> dict:
    key = jax.random.key(seed)
    ks = jax.random.split(key, 24)
    nrm = lambda k, shape, s: jax.random.normal(k, shape, jnp.float32) * s
    L, D = DEPTH, D_MODEL
    return {
        "x_prompt": nrm(ks[0], (BATCH, SEQ, D), 1.0),
        "x_sample": nrm(ks[1], (DEC_BATCH, DEC_SEQ, D), 1.0),
        "g_mix": 1.0 + nrm(ks[2], (L, D), 0.02),
        "w_in": nrm(ks[3], (L, D, IN_WIDTH), D ** -0.5),
        "w_pool": nrm(ks[4], (L, POOL_GROUPS, POOL_GROUP_DIM, POOL_GROUP_DIM), POOL_GROUP_DIM ** -0.5),
        "pool_scale": 1.0 + nrm(ks[5], (L, POOL_WIDTH), 0.02),
        "lambda_q1": nrm(ks[6], (L, QK_HEAD_DIM), 0.1),
        "lambda_k1": nrm(ks[7], (L, QK_HEAD_DIM), 0.1),
        "lambda_q2": nrm(ks[8], (L, QK_HEAD_DIM), 0.1),
        "lambda_k2": nrm(ks[9], (L, QK_HEAD_DIM), 0.1),
        "subln_g": 1.0 + nrm(ks[10], (L, V_HEAD_DIM), 0.02),
        "w_out": nrm(ks[11], (L, MIX_WIDTH, D), MIX_WIDTH ** -0.5),
        "g_ffn": 1.0 + nrm(ks[12], (L, D), 0.02),
        "w_router_group": nrm(ks[13], (L, D, MOE_GROUPS), D ** -0.5),
        "b_router_group": nrm(ks[14], (L, MOE_GROUPS), 0.01),
        "w_router_expert": nrm(ks[15], (L, D, N_EXPERTS), D ** -0.5),
        "b_router_expert": nrm(ks[16], (L, N_EXPERTS), 0.01),
        "w_gate": nrm(ks[17], (L, N_EXPERTS, D, D_FF_EXPERT), D ** -0.5),
        "w_up": nrm(ks[18], (L, N_EXPERTS, D, D_FF_EXPERT), D ** -0.5),
        "w_down": nrm(ks[19], (L, N_EXPERTS, D_FF_EXPERT, D), D_FF_EXPERT ** -0.5),
        "g_final": 1.0 + nrm(ks[20], (D,), 0.02),
    }


def reference(x_prompt, x_sample, g_mix, w_in, w_pool, pool_scale, lambda_q1, lambda_k1,
              lambda_q2, lambda_k2, subln_g, w_out, g_ffn, w_router_group, b_router_group,
              w_router_expert, b_router_expert, w_gate, w_up, w_down, g_final):
    y_prompt = run_trunk(x_prompt, g_mix, w_in, w_pool, pool_scale, lambda_q1, lambda_k1,
                         lambda_q2, lambda_k2, subln_g, w_out, g_ffn, w_router_group,
                         b_router_group, w_router_expert, b_router_expert, w_gate, w_up,
                         w_down, g_final)
    y_sample = run_trunk(x_sample, g_mix, w_in, w_pool, pool_scale, lambda_q1, lambda_k1,
                         lambda_q2, lambda_k2, subln_g, w_out, g_ffn, w_router_group,
                         b_router_group, w_router_expert, b_router_expert, w_gate, w_up,
                         w_down, g_final)
    return (y_prompt, y_sample)
```

```python
import functools

import jax
import jax.numpy as jnp
from jax import lax
from jax.experimental import pallas as pl
from jax.experimental.pallas import tpu as pltpu

F32, BF16, I32 = jnp.float32, jnp.bfloat16, jnp.int32

EPS = 1e-6
ROPE_THETA = 10000.0
POOL_WINDOWS = (2, 4, 8, 16)
POOL_GROUP_DIM = 128
POOL_WIDTH = POOL_GROUP_DIM * len(POOL_WINDOWS)
HEADS = 4
QK_DIM = 64
HEAD_W = 2 * QK_DIM
ATTN_WIDTH = HEADS * HEAD_W
MOE_GROUPS = 4
EXPERTS_PER_GROUP = 8
PAIRS_PER_GROUP = EXPERTS_PER_GROUP * (EXPERTS_PER_GROUP - 1) // 2
N_CLASSES = MOE_GROUPS * PAIRS_PER_GROUP
ROUTER_ROWS = 48

SUBLANES = 8
LANES = 128
VMEM_LIMIT = 48 * 1024 * 1024

TOKEN_TILE = 512
Q_TILE = 256
KV_CHUNK = 512
ROW_BLOCK = 128

V_VALID, V_FIRST, V_LAST, V_FIN, V_NEXT = 1, 2, 4, 8, 16

_NT = (((1,), (1,)), ((), ()))


def _rms(x, g):
    return x * lax.rsqrt(jnp.mean(x * x, axis=-1, keepdims=True) + EPS) * g


def _inproj_kernel(x_ref, g_ref, w_ref, cos_ref, sin_ref, zp_ref, q_ref, k_ref, v_ref):
    xb = _rms(x_ref[...], g_ref[...]).astype(BF16)
    zp_ref[...] = jnp.dot(xb, w_ref[:, 0:POOL_WIDTH], preferred_element_type=F32)
    q0, k0, v0 = POOL_WIDTH, POOL_WIDTH + ATTN_WIDTH, POOL_WIDTH + 2 * ATTN_WIDTH
    v_ref[...] = jnp.dot(xb, w_ref[:, v0:v0 + ATTN_WIDTH], preferred_element_type=F32).astype(BF16)
    cos, sin = cos_ref[...], sin_ref[...]
    lane = lax.broadcasted_iota(I32, cos.shape, 1)
    lower = (lane % QK_DIM) < (QK_DIM // 2)
    half = QK_DIM // 2

    def rope(z):
        partner = jnp.where(lower, pltpu.roll(z, HEAD_W - half, 1), pltpu.roll(z, half, 1))
        return z * cos + partner * sin

    zq = jnp.dot(xb, w_ref[:, q0:q0 + ATTN_WIDTH], preferred_element_type=F32)
    zk = jnp.dot(xb, w_ref[:, k0:k0 + ATTN_WIDTH], preferred_element_type=F32)
    scale = QK_DIM ** -0.5
    for h in range(HEADS):
        sl = slice(h * HEAD_W, (h + 1) * HEAD_W)
        q_ref[:, sl] = (rope(zq[:, sl]) * scale).astype(BF16)
        k_ref[:, sl] = rope(zk[:, sl]).astype(BF16)


def _inproj(x2, g_mix, w_in, cos, sin, seq):
    t, d = x2.shape
    tt = TOKEN_TILE
    per_seq = seq // tt
    row = lambda i: (i, 0)
    return pl.pallas_call(
        _inproj_kernel,
        grid=(t // tt,),
        in_specs=[
            pl.BlockSpec((tt, d), row),
            pl.BlockSpec((1, d), lambda i: (0, 0)),
            pl.BlockSpec(w_in.shape, lambda i: (0, 0)),
            pl.BlockSpec((tt, HEAD_W), lambda i: (i % per_seq, 0)),
            pl.BlockSpec((tt, HEAD_W), lambda i: (i % per_seq, 0)),
        ],
        out_specs=[
            pl.BlockSpec((tt, POOL_WIDTH), row),
            pl.BlockSpec((tt, ATTN_WIDTH), row),
            pl.BlockSpec((tt, ATTN_WIDTH), row),
            pl.BlockSpec((tt, ATTN_WIDTH), row),
        ],
        out_shape=[
            jax.ShapeDtypeStruct((t, POOL_WIDTH), F32),
            jax.ShapeDtypeStruct((t, ATTN_WIDTH), BF16),
            jax.ShapeDtypeStruct((t, ATTN_WIDTH), BF16),
            jax.ShapeDtypeStruct((t, ATTN_WIDTH), BF16),
        ],
        compiler_params=pltpu.CompilerParams(
            dimension_semantics=("arbitrary",), vmem_limit_bytes=VMEM_LIMIT),
        name="inproj",
    )(x2, g_mix, w_in, cos, sin)


def _attn_kernel(lam_ref, q_ref, k_ref, v_ref, sg_ref, o_ref, *, n_chunks, lam_init):
    tq = q_ref.shape[1]
    lp = lam_ref[...]
    lam = (jnp.exp(jnp.sum(lp[0:1] * lp[1:2], axis=-1, keepdims=True))
           - jnp.exp(jnp.sum(lp[2:3] * lp[3:4], axis=-1, keepdims=True)) + lam_init)
    q = q_ref[0]
    lane = lax.broadcasted_iota(I32, q.shape, 1)
    zero = jnp.zeros_like(q)
    q2 = jnp.concatenate([jnp.where(lane < QK_DIM, q, zero), jnp.where(lane >= QK_DIM, q, zero)], axis=0)

    def body(c, carry):
        m, l, acc = carry
        off = pl.multiple_of(c * KV_CHUNK, KV_CHUNK)
        kc = k_ref[0, pl.ds(off, KV_CHUNK), :]
        vc = v_ref[0, pl.ds(off, KV_CHUNK), :]
        s = lax.dot_general(q2, kc, _NT, preferred_element_type=F32)
        m_new = jnp.maximum(m, jnp.max(s, axis=1, keepdims=True))
        alpha = jnp.exp(m - m_new)
        p = jnp.exp(s - m_new)
        l = alpha * l + jnp.sum(p, axis=1, keepdims=True)
        acc = alpha * acc + jnp.dot(p.astype(BF16), vc, preferred_element_type=F32)
        return m_new, l, acc

    m0 = jnp.full((2 * tq, 1), -jnp.inf, F32)
    l0 = jnp.zeros((2 * tq, 1), F32)
    a0 = jnp.zeros((2 * tq, HEAD_W), F32)
    _, l, acc = lax.fori_loop(0, n_chunks, body, (m0, l0, a0))
    o = acc / l
    od = o[:tq] - lam * o[tq:]
    o_ref[0] = (_rms(od, sg_ref[...]) * (1.0 - lam_init)).astype(BF16)


def _attention(q, k, v, lam_params, subln_g, lam_init):
    b, n, _ = q.shape
    kernel = functools.partial(_attn_kernel, n_chunks=n // KV_CHUNK, lam_init=lam_init)
    kv_spec = pl.BlockSpec((1, n, HEAD_W), lambda bi, h, i: (bi, 0, h))
    q_spec = pl.BlockSpec((1, Q_TILE, HEAD_W), lambda bi, h, i: (bi, i, h))
    return pl.pallas_call(
        kernel,
        grid=(b, HEADS, n // Q_TILE),
        in_specs=[
            pl.BlockSpec(lam_params.shape, lambda bi, h, i: (0, 0)),
            q_spec, kv_spec, kv_spec,
            pl.BlockSpec((1, HEAD_W), lambda bi, h, i: (0, 0)),
        ],
        out_specs=q_spec,
        out_shape=jax.ShapeDtypeStruct((b, n, ATTN_WIDTH), BF16),
        compiler_params=pltpu.CompilerParams(
            dimension_semantics=("arbitrary", "arbitrary", "arbitrary"), vmem_limit_bytes=VMEM_LIMIT),
        name="diff_attention",
    )(lam_params, q, k, v, subln_g)


def _first_argmax(vals, best, rows):
    return jnp.min(jnp.where(vals == best, rows, SUBLANES), axis=0, keepdims=True)


def _mix_kernel(x_ref, zp_ref, zprev_ref, znext_ref, at_ref, wout_ref, wpool_ref, ps_ref, gf_ref,
                wrh_ref, wrl_ref, rb_ref, h_ref, cls_ref, gate_ref, *, seq):
    tt = x_ref.shape[0]
    pos0 = (pl.program_id(0) * tt) % seq
    halo = SUBLANES
    zprev = jnp.where(pos0 == 0, 0.0, zprev_ref[...])
    znext = jnp.where(pos0 + tt == seq, 0.0, znext_ref[...])
    zext = jnp.concatenate([zprev, zp_ref[...], znext], axis=0)
    pos = pos0 + lax.broadcasted_iota(I32, (tt, POOL_GROUP_DIM), 0)
    pooled = []
    for g, w in enumerate(POOL_WINDOWS):
        hf = w // 2
        zg = zext[:, g * POOL_GROUP_DIM:(g + 1) * POOL_GROUP_DIM]
        win = zg[halo - hf:halo - hf + tt]
        for s in range(-hf + 1, hf):
            win = win + zg[halo + s:halo + s + tt]
        cnt = (jnp.minimum(pos + hf, seq) - jnp.maximum(pos - hf, 0)).astype(F32)
        pl_g = (win / cnt - zg[halo:halo + tt]).astype(BF16)
        po = jnp.dot(pl_g, wpool_ref[g], preferred_element_type=F32)
        pooled.append((po * ps_ref[:, g * POOL_GROUP_DIM:(g + 1) * POOL_GROUP_DIM]).astype(BF16))
    mix = jnp.concatenate(pooled + [at_ref[...]], axis=1)
    h = x_ref[...] + jnp.dot(mix, wout_ref[...], preferred_element_type=F32)
    h_ref[...] = h

    hn = _rms(h, gf_ref[...])
    hi = hn.astype(BF16)
    lo = (hn - hi.astype(F32)).astype(BF16)
    wrh, wrl = wrh_ref[...], wrl_ref[...]
    lt = (lax.dot_general(wrh, hi, _NT, preferred_element_type=F32)
          + lax.dot_general(wrl, hi, _NT, preferred_element_type=F32)
          + lax.dot_general(wrh, lo, _NT, preferred_element_type=F32)) + rb_ref[...]
    rows = lax.broadcasted_iota(I32, (SUBLANES, tt), 0)
    lg = jnp.where(rows < MOE_GROUPS, lt[0:SUBLANES], -jnp.inf)
    eg = jnp.exp(lg - jnp.max(lg, axis=0, keepdims=True))
    pg = eg / jnp.sum(eg, axis=0, keepdims=True)
    g_top = jnp.max(pg, axis=0, keepdims=True)
    g_idx = _first_argmax(pg, g_top, rows)
    le = lt[SUBLANES:2 * SUBLANES]
    for g in range(1, MOE_GROUPS):
        le = jnp.where(g_idx == g, lt[SUBLANES * (g + 1):SUBLANES * (g + 2)], le)
    ee = jnp.exp(le - jnp.max(le, axis=0, keepdims=True))
    pe = ee / jnp.sum(ee, axis=0, keepdims=True)
    e1 = jnp.max(pe, axis=0, keepdims=True)
    i1 = _first_argmax(pe, e1, rows)
    pe2 = jnp.where(rows == i1, -1.0, pe)
    e2 = jnp.max(pe2, axis=0, keepdims=True)
    i2 = _first_argmax(pe2, e2, rows)
    den = e1 + e2
    gate1 = g_top * (e1 / den)
    gate2 = g_top * (e2 / den)
    a = jnp.minimum(i1, i2)
    b = jnp.maximum(i1, i2)
    pair = ((a * (2 * EXPERTS_PER_GROUP - 1 - a)) >> 1) + (b - a - 1)
    cls_ref[0] = g_idx * PAIRS_PER_GROUP + pair
    first_lower = i1 < i2
    gate_ref[0] = jnp.concatenate(
        [jnp.where(first_lower, gate1, gate2), jnp.where(first_lower, gate2, gate1)], axis=0)


def _mix(x2, zp, attn, w_out, w_pool, pool_scale, g_ffn, wr_hi, wr_lo, r_bias, seq):
    t, d = x2.shape
    tt = TOKEN_TILE
    nt = t // tt
    hb = tt // SUBLANES
    last_halo = t // SUBLANES - 1
    row = lambda i: (i, 0)
    const2 = lambda i: (0, 0)
    kernel = functools.partial(_mix_kernel, seq=seq)
    return pl.pallas_call(
        kernel,
        grid=(nt,),
        in_specs=[
            pl.BlockSpec((tt, d), row),
            pl.BlockSpec((tt, POOL_WIDTH), row),
            pl.BlockSpec((SUBLANES, POOL_WIDTH), lambda i: (jnp.maximum(i * hb - 1, 0), 0)),
            pl.BlockSpec((SUBLANES, POOL_WIDTH), lambda i: (jnp.minimum((i + 1) * hb, last_halo), 0)),
            pl.BlockSpec((tt, ATTN_WIDTH), row),
            pl.BlockSpec(w_out.shape, const2),
            pl.BlockSpec(w_pool.shape, lambda i: (0, 0, 0)),
            pl.BlockSpec((1, POOL_WIDTH), const2),
            pl.BlockSpec((1, d), const2),
            pl.BlockSpec(wr_hi.shape, const2),
            pl.BlockSpec(wr_lo.shape, const2),
            pl.BlockSpec(r_bias.shape, const2),
        ],
        out_specs=[
            pl.BlockSpec((tt, d), row),
            pl.BlockSpec((1, 1, tt), lambda i: (i, 0, 0)),
            pl.BlockSpec((1, 2, tt), lambda i: (i, 0, 0)),
        ],
        out_shape=[
            jax.ShapeDtypeStruct((t, d), F32),
            jax.ShapeDtypeStruct((nt, 1, tt), I32),
            jax.ShapeDtypeStruct((nt, 2, tt), F32),
        ],
        compiler_params=pltpu.CompilerParams(
            dimension_semantics=("arbitrary",), vmem_limit_bytes=VMEM_LIMIT),
        name="mix_router",
    )(x2, zp, zp, zp, attn, w_out, w_pool, pool_scale, g_ffn, wr_hi, wr_lo, r_bias)


def _moe_kernel(vblk_ref, vea_ref, veb_ref, vlo_ref, vhi_ref, vfl_ref,
                tokc_ref, tokn_ref, gates_ref, gf_ref, gfin_ref,
                wga_ref, wua_ref, wda_ref, wgb_ref, wub_ref, wdb_ref, h_hbm,
                out_hbm, hbuf, xnbuf, acc, obuf, gsem, ssem):
    v = pl.program_id(0)
    flags = vfl_ref[v]
    rb = hbuf.shape[1]

    def row_copy_in(tok_ref, r, slot):
        return pltpu.make_async_copy(h_hbm.at[pl.ds(tok_ref[0, 0, r], 1), :],
                                     hbuf.at[slot, pl.ds(r, 1), :], gsem.at[slot])

    def row_copy_out(r, slot):
        return pltpu.make_async_copy(obuf.at[slot, pl.ds(r, 1), :],
                                     out_hbm.at[pl.ds(tokc_ref[0, 0, r], 1), :], ssem.at[slot])

    def start_gather(tok_ref, slot):
        def body(r, c):
            row_copy_in(tok_ref, r, slot).start()
            return c
        lax.fori_loop(0, rb, body, 0, unroll=8)

    def wait_gather(slot):
        def body(r, c):
            row_copy_in(tokc_ref, r, slot).wait()
            return c
        lax.fori_loop(0, rb, body, 0, unroll=8)

    def start_scatter(slot):
        def body(r, c):
            row_copy_out(r, slot).start()
            return c
        lax.fori_loop(0, rb, body, 0, unroll=8)

    def wait_scatter(slot):
        def body(r, c):
            row_copy_out(r, slot).wait()
            return c
        lax.fori_loop(0, rb, body, 0, unroll=8)

    @pl.when((flags & V_VALID) != 0)
    def _visit():
        blk = vblk_ref[v]
        slot = blk % 2

        @pl.when((flags & V_FIRST) != 0)
        def _first():
            @pl.when(v == 0)
            def _():
                start_gather(tokc_ref, slot)
            wait_gather(slot)

            @pl.when((flags & V_NEXT) != 0)
            def _():
                start_gather(tokn_ref, 1 - slot)
            xnbuf[...] = _rms(hbuf[slot], gf_ref[...]).astype(BF16)
            acc[...] = jnp.zeros_like(acc)

        xn = xnbuf[...]
        rows = lax.broadcasted_iota(I32, (rb, 1), 0)
        seg = (rows >= vlo_ref[v]) & (rows < vhi_ref[v])
        gts = gates_ref[...]

        def hidden(wg_ref, wu_ref, gate):
            hid = (jax.nn.silu(jnp.dot(xn, wg_ref[0], preferred_element_type=F32))
                   * jnp.dot(xn, wu_ref[0], preferred_element_type=F32))
            return jnp.where(seg, hid * gate, 0.0).astype(BF16)

        ha = hidden(wga_ref, wua_ref, gts[:, 0:1])
        hb = hidden(wgb_ref, wub_ref, gts[:, LANES // 2:LANES // 2 + 1])
        acc[...] += (jnp.dot(ha, wda_ref[0], preferred_element_type=F32)
                     + jnp.dot(hb, wdb_ref[0], preferred_element_type=F32))

        @pl.when((flags & V_LAST) != 0)
        def _last():
            y = _rms(hbuf[slot] + acc[...], gfin_ref[...])

            @pl.when(blk >= 2)
            def _():
                wait_scatter(slot)
            obuf[slot] = y
            start_scatter(slot)

        @pl.when((flags & V_FIN) != 0)
        def _fin():
            wait_scatter(slot)

            @pl.when(blk >= 1)
            def _():
                wait_scatter(1 - slot)


def _moe(plan, h, gates_rows, g_ffn, g_final, w_gate, w_up, w_down):
    t, d = h.shape
    rb = ROW_BLOCK
    n_blk = t // rb
    n_vis = plan["blk"].shape[0]
    dff = w_gate.shape[2]
    cur = lambda v, blk, *_: (blk[v], 0)
    tok_cur = lambda v, blk, *_: (blk[v], 0, 0)
    tok_nxt = lambda v, blk, *_: (jnp.minimum(blk[v] + 1, n_blk - 1), 0, 0)
    const2 = lambda v, *_: (0, 0)
    wa = lambda v, blk, ea, *_: (ea[v], 0, 0)
    wb = lambda v, blk, ea, eb, *_: (eb[v], 0, 0)
    grid_spec = pltpu.PrefetchScalarGridSpec(
        num_scalar_prefetch=6,
        grid=(n_vis,),
        in_specs=[
            pl.BlockSpec((1, 1, rb), tok_cur, memory_space=pltpu.SMEM),
            pl.BlockSpec((1, 1, rb), tok_nxt, memory_space=pltpu.SMEM),
            pl.BlockSpec((rb, LANES), cur),
            pl.BlockSpec((1, d), const2),
            pl.BlockSpec((1, d), const2),
            pl.BlockSpec((1, d, dff), wa),
            pl.BlockSpec((1, d, dff), wa),
            pl.BlockSpec((1, dff, d), wa),
            pl.BlockSpec((1, d, dff), wb),
            pl.BlockSpec((1, d, dff), wb),
            pl.BlockSpec((1, dff, d), wb),
            pl.BlockSpec(memory_space=pl.ANY),
        ],
        out_specs=pl.BlockSpec(memory_space=pl.ANY),
        scratch_shapes=[
            pltpu.VMEM((2, rb, d), F32),
            pltpu.VMEM((rb, d), BF16),
            pltpu.VMEM((rb, d), F32),
            pltpu.VMEM((2, rb, d), F32),
            pltpu.SemaphoreType.DMA((2,)),
            pltpu.SemaphoreType.DMA((2,)),
        ],
    )
    return pl.pallas_call(
        _moe_kernel,
        grid_spec=grid_spec,
        out_shape=jax.ShapeDtypeStruct((t, d), F32),
        compiler_params=pltpu.CompilerParams(
            dimension_semantics=("arbitrary",), vmem_limit_bytes=VMEM_LIMIT),
        name="moe",
    )(plan["blk"], plan["ea"], plan["eb"], plan["lo"], plan["hi"], plan["flags"],
      plan["row_tok"], plan["row_tok"], gates_rows, g_ffn, g_final,
      w_gate, w_up, w_down, w_gate, w_up, w_down, h)


def _pair_tables():
    lo, hi = [], []
    for a in range(EXPERTS_PER_GROUP):
        for b in range(a + 1, EXPERTS_PER_GROUP):
            lo.append(a)
            hi.append(b)
    return jnp.array(lo, I32), jnp.array(hi, I32)


def _dispatch_plan(cls, t):
    rb = ROW_BLOCK
    n_blk = t // rb
    n_vis = n_blk + N_CLASSES - 1
    order = jnp.argsort(cls, stable=True).astype(I32)
    counts = jnp.bincount(cls, length=N_CLASSES).astype(I32)
    cend = jnp.cumsum(counts)
    cstart = cend - counts
    first_blk = cstart // rb
    last_blk = jnp.maximum(cend - 1, 0) // rb
    nvis = jnp.where(counts > 0, last_blk - first_blk + 1, 0)
    vend = jnp.cumsum(nvis)
    vstart = vend - nvis
    total = vend[-1]
    vid = jnp.arange(n_vis, dtype=I32)
    vc = jnp.minimum(vid, total - 1)
    c = jnp.searchsorted(vend, vc, side="right").astype(I32)
    blk = first_blk[c] + (vc - vstart[c])
    lo = jnp.maximum(cstart[c], blk * rb) - blk * rb
    hi = jnp.minimum(cend[c], (blk + 1) * rb) - blk * rb
    valid = vid < total
    prev_blk = jnp.concatenate([jnp.full((1,), -1, I32), blk[:-1]])
    next_blk = jnp.concatenate([blk[1:], jnp.full((1,), -1, I32)])
    first = valid & (blk != prev_blk)
    last = valid & ((blk != next_blk) | (vid == total - 1))
    fin = vid == total - 1
    has_next = blk < n_blk - 1
    flags = (valid * V_VALID + first * V_FIRST + last * V_LAST + fin * V_FIN + has_next * V_NEXT).astype(I32)
    pair_lo, pair_hi = _pair_tables()
    grp = c // PAIRS_PER_GROUP
    pidx = c % PAIRS_PER_GROUP
    plan = dict(blk=blk.astype(I32), ea=(grp * EXPERTS_PER_GROUP + pair_lo[pidx]).astype(I32),
                eb=(grp * EXPERTS_PER_GROUP + pair_hi[pidx]).astype(I32),
                lo=lo.astype(I32), hi=hi.astype(I32), flags=flags,
                row_tok=order.reshape(n_blk, 1, rb))
    return plan, order


def _rope_tables(n):
    inv_freq = ROPE_THETA ** (-jnp.arange(0, QK_DIM, 2, dtype=F32) / QK_DIM)
    ang = jnp.arange(n, dtype=F32)[:, None] * inv_freq[None, :]
    cos, sin = jnp.cos(ang), jnp.sin(ang)
    cos_t = jnp.tile(cos, (1, 2 * HEAD_W // QK_DIM))
    sin_t = jnp.tile(jnp.concatenate([-sin, sin], axis=1), (1, HEAD_W // QK_DIM))
    return cos_t, sin_t


def _trunk(x, p, lam_init):
    b, n, d = x.shape
    t = b * n
    assert n % TOKEN_TILE == 0 and n % Q_TILE == 0 and n % KV_CHUNK == 0 and t % ROW_BLOCK == 0
    x2 = x.reshape(t, d)
    cos, sin = _rope_tables(n)
    zp, q, k, v = _inproj(x2, p["g_mix"], p["w_in"], cos, sin, n)
    attn = _attention(q.reshape(b, n, ATTN_WIDTH), k.reshape(b, n, ATTN_WIDTH), v.reshape(b, n, ATTN_WIDTH),
                      p["lam_params"], p["subln_g"], lam_init)
    h, cls3, gates3 = _mix(x2, zp, attn.reshape(t, ATTN_WIDTH), p["w_out"], p["w_pool"], p["pool_scale"],
                           p["g_ffn"], p["wr_hi"], p["wr_lo"], p["r_bias"], n)
    cls = cls3.reshape(t)
    gates = jnp.transpose(gates3, (1, 0, 2)).reshape(2, t)
    plan, order = _dispatch_plan(cls, t)
    gs = gates[:, order]
    gates_rows = jnp.concatenate([jnp.broadcast_to(gs[0][:, None], (t, LANES // 2)),
                                  jnp.broadcast_to(gs[1][:, None], (t, LANES // 2))], axis=1)
    y = _moe(plan, h, gates_rows, p["g_ffn"], p["g_final"], p["w_gate"], p["w_up"], p["w_down"])
    return y.reshape(b, n, d)


def kernel(x_prompt, x_sample, g_mix, w_in, w_pool, pool_scale, lambda_q1, lambda_k1, lambda_q2, lambda_k2,
           subln_g, w_out, g_ffn, w_router_group, b_router_group, w_router_expert, b_router_expert,
           w_gate, w_up, w_down, g_final):
    depth = g_mix.shape[0]
    assert depth == 1
    l = 0
    lam_init = 0.8 - 0.6 * 1.0
    d = g_mix.shape[1]
    wr = jnp.zeros((ROUTER_ROWS, d), F32)
    wr = wr.at[0:MOE_GROUPS].set(w_router_group[l].T)
    wr = wr.at[SUBLANES:SUBLANES + MOE_GROUPS * EXPERTS_PER_GROUP].set(w_router_expert[l].T)
    wr_hi = wr.astype(BF16)
    wr_lo = (wr - wr_hi.astype(F32)).astype(BF16)
    r_bias = jnp.zeros((ROUTER_ROWS, 1), F32)
    r_bias = r_bias.at[0:MOE_GROUPS, 0].set(b_router_group[l])
    r_bias = r_bias.at[SUBLANES:SUBLANES + MOE_GROUPS * EXPERTS_PER_GROUP, 0].set(b_router_expert[l])
    p = dict(
        g_mix=g_mix[l][None, :], w_in=w_in[l].astype(BF16), w_pool=w_pool[l].astype(BF16),
        pool_scale=pool_scale[l][None, :],
        lam_params=jnp.stack([lambda_q1[l], lambda_k1[l], lambda_q2[l], lambda_k2[l]]),
        subln_g=subln_g[l][None, :], w_out=w_out[l].astype(BF16), g_ffn=g_ffn[l][None, :],
        wr_hi=wr_hi, wr_lo=wr_lo, r_bias=r_bias,
        w_gate=w_gate[l].astype(BF16), w_up=w_up[l].astype(BF16), w_down=w_down[l].astype(BF16),
        g_final=g_final[None, :],
    )
    return (_trunk(x_prompt, p, lam_init), _trunk(x_sample, p, lam_init))
```

```python
import functools

import jax
import jax.numpy as jnp
from jax import lax
from jax.experimental import pallas as pl
from jax.experimental.pallas import tpu as pltpu

F32, BF16, I32 = jnp.float32, jnp.bfloat16, jnp.int32

EPS = 1e-6
ROPE_THETA = 10000.0
LOG2_E = 1.4426950408889634
POOL_WINDOWS = (2, 4, 8, 16)
POOL_GROUP_DIM = 128
POOL_WIDTH = POOL_GROUP_DIM * len(POOL_WINDOWS)
HEADS = 4
QK_DIM = 64
HEAD_W = 2 * QK_DIM
ATTN_WIDTH = HEADS * HEAD_W
MOE_GROUPS = 4
EXPERTS_PER_GROUP = 8
PAIRS_PER_GROUP = EXPERTS_PER_GROUP * (EXPERTS_PER_GROUP - 1) // 2
N_CLASSES = MOE_GROUPS * PAIRS_PER_GROUP
ROUTER_ROWS = 48

SUBLANES = 8
LANES = 128
MXU_N = 256
VMEM_LIMIT = 48 * 1024 * 1024

TOKEN_TILE = 512
Q_TILE = 256
KV_CHUNK = 1024
SCORE_LOOKAHEAD = 2
ROW_BLOCK = 128

V_VALID, V_FIRST, V_LAST, V_FIN, V_NEXT = 1, 2, 4, 8, 16

_NT = (((1,), (1,)), ((), ()))


def _rms(x, g):
    return x * lax.rsqrt(jnp.mean(x * x, axis=-1, keepdims=True) + EPS) * g


def _inproj_kernel(x_ref, g_ref, w_ref, wvt_ref, cos_ref, sin_ref, zp_ref, q_ref, k_ref, vt_ref):
    xb = _rms(x_ref[...], g_ref[...]).astype(BF16)
    zp_ref[...] = jnp.dot(xb, w_ref[:, 0:POOL_WIDTH], preferred_element_type=F32)
    q0, k0 = POOL_WIDTH, POOL_WIDTH + ATTN_WIDTH
    vt_ref[0] = lax.dot_general(wvt_ref[...], xb, _NT, preferred_element_type=F32).astype(BF16)
    cos, sin = cos_ref[...], sin_ref[...]
    lane = lax.broadcasted_iota(I32, cos.shape, 1)
    lower = (lane % QK_DIM) < (QK_DIM // 2)
    half = QK_DIM // 2

    def rope(z):
        partner = jnp.where(lower, pltpu.roll(z, HEAD_W - half, 1), pltpu.roll(z, half, 1))
        return z * cos + partner * sin

    zq = jnp.dot(xb, w_ref[:, q0:q0 + ATTN_WIDTH], preferred_element_type=F32)
    zk = jnp.dot(xb, w_ref[:, k0:k0 + ATTN_WIDTH], preferred_element_type=F32)
    scale = QK_DIM ** -0.5 * LOG2_E
    for h in range(HEADS):
        sl = slice(h * HEAD_W, (h + 1) * HEAD_W)
        q_ref[:, sl] = (rope(zq[:, sl]) * scale).astype(BF16)
        k_ref[:, sl] = rope(zk[:, sl]).astype(BF16)


def _inproj(x2, g_mix, w_in, w_vt, cos, sin, seq):
    t, d = x2.shape
    tt = TOKEN_TILE
    per_seq = seq // tt
    row = lambda i: (i, 0)
    return pl.pallas_call(
        _inproj_kernel,
        grid=(t // tt,),
        in_specs=[
            pl.BlockSpec((tt, d), row),
            pl.BlockSpec((1, d), lambda i: (0, 0)),
            pl.BlockSpec(w_in.shape, lambda i: (0, 0)),
            pl.BlockSpec(w_vt.shape, lambda i: (0, 0)),
            pl.BlockSpec((tt, HEAD_W), lambda i: (i % per_seq, 0)),
            pl.BlockSpec((tt, HEAD_W), lambda i: (i % per_seq, 0)),
        ],
        out_specs=[
            pl.BlockSpec((tt, POOL_WIDTH), row),
            pl.BlockSpec((tt, ATTN_WIDTH), row),
            pl.BlockSpec((tt, ATTN_WIDTH), row),
            pl.BlockSpec((1, ATTN_WIDTH, tt), lambda i: (i // per_seq, 0, i % per_seq)),
        ],
        out_shape=[
            jax.ShapeDtypeStruct((t, POOL_WIDTH), F32),
            jax.ShapeDtypeStruct((t, ATTN_WIDTH), BF16),
            jax.ShapeDtypeStruct((t, ATTN_WIDTH), BF16),
            jax.ShapeDtypeStruct((t // seq, ATTN_WIDTH, seq), BF16),
        ],
        compiler_params=pltpu.CompilerParams(
            dimension_semantics=("arbitrary",), vmem_limit_bytes=VMEM_LIMIT),
        name="inproj",
    )(x2, g_mix, w_in, w_vt, cos, sin)


def _attn_kernel(lam_ref, q_ref, k_ref, vt_ref, sg_ref, o_ref, *, n_chunks, lam_init):
    tq = q_ref.shape[1]
    lp = lam_ref[...]
    lam = (jnp.exp(jnp.sum(lp[0:1] * lp[1:2], axis=-1, keepdims=True))
           - jnp.exp(jnp.sum(lp[2:3] * lp[3:4], axis=-1, keepdims=True)) + lam_init)
    q = q_ref[0]
    lane = lax.broadcasted_iota(I32, q.shape, 1)
    zero = jnp.zeros_like(q)
    q2 = jnp.concatenate([jnp.where(lane < QK_DIM, q, zero), jnp.where(lane >= QK_DIM, q, zero)], axis=0)

    n_tiles = 2 * tq // MXU_N
    groups = MXU_N // LANES
    m = [jnp.full((1, LANES), -jnp.inf, F32) for _ in range(n_tiles * groups)]
    lsum = [jnp.zeros((SUBLANES, LANES), F32) for _ in range(n_tiles * groups)]
    acc = [jnp.zeros((HEAD_W, MXU_N), F32) for _ in range(n_tiles)]

    def scores(c):
        kc = k_ref[0, c * KV_CHUNK:(c + 1) * KV_CHUNK, :]
        return [lax.dot_general(kc, q2[t * MXU_N:(t + 1) * MXU_N], _NT, preferred_element_type=F32)
                for t in range(n_tiles)]

    pending = [scores(c) for c in range(min(SCORE_LOOKAHEAD, n_chunks))]
    for c in range(n_chunks):
        st = pending.pop(0)
        if c + SCORE_LOOKAHEAD < n_chunks:
            pending.append(scores(c + SCORE_LOOKAHEAD))
        vtc = vt_ref[0, :, c * KV_CHUNK:(c + 1) * KV_CHUNK]
        for t in range(n_tiles):
            ps, alphas = [], []
            for g in range(groups):
                j = t * groups + g
                s = st[t][:, g * LANES:(g + 1) * LANES]
                m_new = jnp.maximum(m[j], jnp.max(s, axis=0, keepdims=True))
                alpha = jnp.exp2(m[j] - m_new)
                p = jnp.exp2(s - m_new)
                lsum[j] = alpha * lsum[j] + jnp.sum(p.reshape(KV_CHUNK // SUBLANES, SUBLANES, LANES), axis=0)
                m[j] = m_new
                ps.append(p.astype(BF16))
                alphas.append(alpha)
            acc[t] = (jnp.concatenate(alphas, axis=1) * acc[t]
                      + jnp.dot(vtc, jnp.concatenate(ps, axis=1), preferred_element_type=F32))
    acc = jnp.concatenate(acc, axis=1)
    lsum = jnp.concatenate(lsum, axis=1)
    o = acc / jnp.sum(lsum, axis=0, keepdims=True)
    od = o[:, :tq] - lam * o[:, tq:]
    y = od * lax.rsqrt(jnp.mean(od * od, axis=0, keepdims=True) + EPS) * sg_ref[...] * (1.0 - lam_init)
    o_ref[0] = y.T.astype(BF16)


def _attention(q, k, vt, lam_params, subln_g_col, lam_init):
    b, n, _ = q.shape
    kernel = functools.partial(_attn_kernel, n_chunks=n // KV_CHUNK, lam_init=lam_init)
    q_spec = pl.BlockSpec((1, Q_TILE, HEAD_W), lambda bi, h, i: (bi, i, h))
    return pl.pallas_call(
        kernel,
        grid=(b, HEADS, n // Q_TILE),
        in_specs=[
            pl.BlockSpec(lam_params.shape, lambda bi, h, i: (0, 0)),
            q_spec,
            pl.BlockSpec((1, n, HEAD_W), lambda bi, h, i: (bi, 0, h)),
            pl.BlockSpec((1, HEAD_W, n), lambda bi, h, i: (bi, h, 0)),
            pl.BlockSpec((HEAD_W, 1), lambda bi, h, i: (0, 0)),
        ],
        out_specs=q_spec,
        out_shape=jax.ShapeDtypeStruct((b, n, ATTN_WIDTH), BF16),
        compiler_params=pltpu.CompilerParams(
            dimension_semantics=("arbitrary", "arbitrary", "arbitrary"), vmem_limit_bytes=VMEM_LIMIT),
        name="diff_attention",
    )(lam_params, q, k, vt, subln_g_col)


def _first_argmax(vals, best, rows):
    return jnp.min(jnp.where(vals == best, rows, SUBLANES), axis=0, keepdims=True)


def _mix_kernel(x_ref, zp_ref, zprev_ref, znext_ref, at_ref, wout_ref, wpool_ref, ps_ref, gf_ref,
                wrh_ref, wrl_ref, rb_ref, h_ref, cls_ref, gate_ref, *, seq):
    tt = x_ref.shape[0]
    pos0 = (pl.program_id(0) * tt) % seq
    halo = SUBLANES
    zprev = jnp.where(pos0 == 0, 0.0, zprev_ref[...])
    znext = jnp.where(pos0 + tt == seq, 0.0, znext_ref[...])
    zext = jnp.concatenate([zprev, zp_ref[...], znext], axis=0)
    pos = pos0 + lax.broadcasted_iota(I32, (tt, POOL_GROUP_DIM), 0)
    pooled = []
    for g, w in enumerate(POOL_WINDOWS):
        hf = w // 2
        zg = zext[:, g * POOL_GROUP_DIM:(g + 1) * POOL_GROUP_DIM]
        win = zg[halo - hf:halo - hf + tt]
        for s in range(-hf + 1, hf):
            win = win + zg[halo + s:halo + s + tt]
        cnt = (jnp.minimum(pos + hf, seq) - jnp.maximum(pos - hf, 0)).astype(F32)
        pl_g = (win / cnt - zg[halo:halo + tt]).astype(BF16)
        po = jnp.dot(pl_g, wpool_ref[g], preferred_element_type=F32)
        pooled.append((po * ps_ref[:, g * POOL_GROUP_DIM:(g + 1) * POOL_GROUP_DIM]).astype(BF16))
    mix = jnp.concatenate(pooled + [at_ref[...]], axis=1)
    h = x_ref[...] + jnp.dot(mix, wout_ref[...], preferred_element_type=F32)
    h_ref[...] = h

    hn = _rms(h, gf_ref[...])
    hi = hn.astype(BF16)
    lo = (hn - hi.astype(F32)).astype(BF16)
    wrh, wrl = wrh_ref[...], wrl_ref[...]
    lt = (lax.dot_general(wrh, hi, _NT, preferred_element_type=F32)
          + lax.dot_general(wrl, hi, _NT, preferred_element_type=F32)
          + lax.dot_general(wrh, lo, _NT, preferred_element_type=F32)) + rb_ref[...]
    rows = lax.broadcasted_iota(I32, (SUBLANES, tt), 0)
    lg = jnp.where(rows < MOE_GROUPS, lt[0:SUBLANES], -jnp.inf)
    eg = jnp.exp(lg - jnp.max(lg, axis=0, keepdims=True))
    pg = eg / jnp.sum(eg, axis=0, keepdims=True)
    g_top = jnp.max(pg, axis=0, keepdims=True)
    g_idx = _first_argmax(pg, g_top, rows)
    le = lt[SUBLANES:2 * SUBLANES]
    for g in range(1, MOE_GROUPS):
        le = jnp.where(g_idx == g, lt[SUBLANES * (g + 1):SUBLANES * (g + 2)], le)
    ee = jnp.exp(le - jnp.max(le, axis=0, keepdims=True))
    pe = ee / jnp.sum(ee, axis=0, keepdims=True)
    e1 = jnp.max(pe, axis=0, keepdims=True)
    i1 = _first_argmax(pe, e1, rows)
    pe2 = jnp.where(rows == i1, -1.0, pe)
    e2 = jnp.max(pe2, axis=0, keepdims=True)
    i2 = _first_argmax(pe2, e2, rows)
    den = e1 + e2
    gate1 = g_top * (e1 / den)
    gate2 = g_top * (e2 / den)
    a = jnp.minimum(i1, i2)
    b = jnp.maximum(i1, i2)
    pair = ((a * (2 * EXPERTS_PER_GROUP - 1 - a)) >> 1) + (b - a - 1)
    cls_ref[0] = g_idx * PAIRS_PER_GROUP + pair
    first_lower = i1 < i2
    gate_ref[0] = jnp.concatenate(
        [jnp.where(first_lower, gate1, gate2), jnp.where(first_lower, gate2, gate1)], axis=0)


def _mix(x2, zp, attn, w_out, w_pool, pool_scale, g_ffn, wr_hi, wr_lo, r_bias, seq):
    t, d = x2.shape
    tt = TOKEN_TILE
    nt = t // tt
    hb = tt // SUBLANES
    last_halo = t // SUBLANES - 1
    row = lambda i: (i, 0)
    const2 = lambda i: (0, 0)
    kernel = functools.partial(_mix_kernel, seq=seq)
    return pl.pallas_call(
        kernel,
        grid=(nt,),
        in_specs=[
            pl.BlockSpec((tt, d), row),
            pl.BlockSpec((tt, POOL_WIDTH), row),
            pl.BlockSpec((SUBLANES, POOL_WIDTH), lambda i: (jnp.maximum(i * hb - 1, 0), 0)),
            pl.BlockSpec((SUBLANES, POOL_WIDTH), lambda i: (jnp.minimum((i + 1) * hb, last_halo), 0)),
            pl.BlockSpec((tt, ATTN_WIDTH), row),
            pl.BlockSpec(w_out.shape, const2),
            pl.BlockSpec(w_pool.shape, lambda i: (0, 0, 0)),
            pl.BlockSpec((1, POOL_WIDTH), const2),
            pl.BlockSpec((1, d), const2),
            pl.BlockSpec(wr_hi.shape, const2),
            pl.BlockSpec(wr_lo.shape, const2),
            pl.BlockSpec(r_bias.shape, const2),
        ],
        out_specs=[
            pl.BlockSpec((tt, d), row),
            pl.BlockSpec((1, 1, tt), lambda i: (i, 0, 0)),
            pl.BlockSpec((1, 2, tt), lambda i: (i, 0, 0)),
        ],
        out_shape=[
            jax.ShapeDtypeStruct((t, d), F32),
            jax.ShapeDtypeStruct((nt, 1, tt), I32),
            jax.ShapeDtypeStruct((nt, 2, tt), F32),
        ],
        compiler_params=pltpu.CompilerParams(
            dimension_semantics=("arbitrary",), vmem_limit_bytes=VMEM_LIMIT),
        name="mix_router",
    )(x2, zp, zp, zp, attn, w_out, w_pool, pool_scale, g_ffn, wr_hi, wr_lo, r_bias)


def _moe_kernel(vblk_ref, vea_ref, veb_ref, vlo_ref, vhi_ref, vfl_ref,
                tokc_ref, tokn_ref, gates_ref, gf_ref, gfin_ref,
                wga_ref, wua_ref, wda_ref, wgb_ref, wub_ref, wdb_ref, h_hbm,
                out_hbm, hbuf, xnbuf, acc, obuf, gsem, ssem):
    v = pl.program_id(0)
    flags = vfl_ref[v]
    rb = hbuf.shape[1]

    def row_copy_in(tok_ref, r, slot):
        return pltpu.make_async_copy(h_hbm.at[pl.ds(tok_ref[0, 0, r], 1), :],
                                     hbuf.at[slot, pl.ds(r, 1), :], gsem.at[slot])

    def row_copy_out(r, slot):
        return pltpu.make_async_copy(obuf.at[slot, pl.ds(r, 1), :],
                                     out_hbm.at[pl.ds(tokc_ref[0, 0, r], 1), :], ssem.at[slot])

    def start_gather(tok_ref, slot):
        for r in range(rb):
            row_copy_in(tok_ref, r, slot).start(priority=r % 2)

    def wait_gather(slot):
        def body(r, c):
            row_copy_in(tokc_ref, r, slot).wait()
            return c
        lax.fori_loop(0, rb, body, 0, unroll=8)

    def start_scatter(slot):
        for r in range(rb):
            row_copy_out(r, slot).start(priority=r % 2)

    def wait_scatter(slot):
        def body(r, c):
            row_copy_out(r, slot).wait()
            return c
        lax.fori_loop(0, rb, body, 0, unroll=8)

    @pl.when((flags & V_VALID) != 0)
    def _visit():
        blk = vblk_ref[v]
        slot = blk % 2

        @pl.when((flags & V_FIRST) != 0)
        def _first():
            @pl.when(v == 0)
            def _():
                def body(r, c):
                    row_copy_in(tokc_ref, r, slot).start()
                    return c
                lax.fori_loop(0, rb, body, 0)
            wait_gather(slot)

            @pl.when((flags & V_NEXT) != 0)
            def _():
                start_gather(tokn_ref, 1 - slot)
            xnbuf[...] = _rms(hbuf[slot], gf_ref[...]).astype(BF16)
            acc[...] = jnp.zeros_like(acc)

        xn = xnbuf[...]
        rows = lax.broadcasted_iota(I32, (rb, 1), 0)
        seg = (rows >= vlo_ref[v]) & (rows < vhi_ref[v])
        gts = gates_ref[...]

        def hidden(wg_ref, wu_ref, gate):
            hid = (jax.nn.silu(jnp.dot(xn, wg_ref[0], preferred_element_type=F32))
                   * jnp.dot(xn, wu_ref[0], preferred_element_type=F32))
            return jnp.where(seg, hid * gate, 0.0).astype(BF16)

        ha = hidden(wga_ref, wua_ref, gts[:, 0:1])
        hb = hidden(wgb_ref, wub_ref, gts[:, LANES // 2:LANES // 2 + 1])
        acc[...] += (jnp.dot(ha, wda_ref[0], preferred_element_type=F32)
                     + jnp.dot(hb, wdb_ref[0], preferred_element_type=F32))

        @pl.when((flags & V_LAST) != 0)
        def _last():
            y = _rms(hbuf[slot] + acc[...], gfin_ref[...])

            @pl.when(blk >= 2)
            def _():
                wait_scatter(slot)
            obuf[slot] = y
            start_scatter(slot)

        @pl.when((flags & V_FIN) != 0)
        def _fin():
            wait_scatter(slot)

            @pl.when(blk >= 1)
            def _():
                wait_scatter(1 - slot)


def _moe(plan, h, gates_rows, g_ffn, g_final, w_gate, w_up, w_down):
    t, d = h.shape
    rb = ROW_BLOCK
    n_blk = t // rb
    n_vis = plan["blk"].shape[0]
    dff = w_gate.shape[2]
    cur = lambda v, blk, *_: (blk[v], 0)
    tok_cur = lambda v, blk, *_: (blk[v], 0, 0)
    tok_nxt = lambda v, blk, *_: (jnp.minimum(blk[v] + 1, n_blk - 1), 0, 0)
    const2 = lambda v, *_: (0, 0)
    wa = lambda v, blk, ea, *_: (ea[v], 0, 0)
    wb = lambda v, blk, ea, eb, *_: (eb[v], 0, 0)
    grid_spec = pltpu.PrefetchScalarGridSpec(
        num_scalar_prefetch=6,
        grid=(n_vis,),
        in_specs=[
            pl.BlockSpec((1, 1, rb), tok_cur, memory_space=pltpu.SMEM),
            pl.BlockSpec((1, 1, rb), tok_nxt, memory_space=pltpu.SMEM),
            pl.BlockSpec((rb, LANES), cur),
            pl.BlockSpec((1, d), const2),
            pl.BlockSpec((1, d), const2),
            pl.BlockSpec((1, d, dff), wa),
            pl.BlockSpec((1, d, dff), wa),
            pl.BlockSpec((1, dff, d), wa),
            pl.BlockSpec((1, d, dff), wb),
            pl.BlockSpec((1, d, dff), wb),
            pl.BlockSpec((1, dff, d), wb),
            pl.BlockSpec(memory_space=pl.ANY),
        ],
        out_specs=pl.BlockSpec(memory_space=pl.ANY),
        scratch_shapes=[
            pltpu.VMEM((2, rb, d), F32),
            pltpu.VMEM((rb, d), BF16),
            pltpu.VMEM((rb, d), F32),
            pltpu.VMEM((2, rb, d), F32),
            pltpu.SemaphoreType.DMA((2,)),
            pltpu.SemaphoreType.DMA((2,)),
        ],
    )
    return pl.pallas_call(
        _moe_kernel,
        grid_spec=grid_spec,
        out_shape=jax.ShapeDtypeStruct((t, d), F32),
        compiler_params=pltpu.CompilerParams(
            dimension_semantics=("arbitrary",), vmem_limit_bytes=VMEM_LIMIT),
        name="moe",
    )(plan["blk"], plan["ea"], plan["eb"], plan["lo"], plan["hi"], plan["flags"],
      plan["row_tok"], plan["row_tok"], gates_rows, g_ffn, g_final,
      w_gate, w_up, w_down, w_gate, w_up, w_down, h)


def _pair_tables():
    lo, hi = [], []
    for a in range(EXPERTS_PER_GROUP):
        for b in range(a + 1, EXPERTS_PER_GROUP):
            lo.append(a)
            hi.append(b)
    return jnp.array(lo, I32), jnp.array(hi, I32)


def _dispatch_plan(cls, t):
    rb = ROW_BLOCK
    n_blk = t // rb
    n_vis = n_blk + N_CLASSES - 1
    order = jnp.argsort(cls, stable=True).astype(I32)
    counts = jnp.bincount(cls, length=N_CLASSES).astype(I32)
    cend = jnp.cumsum(counts)
    cstart = cend - counts
    first_blk = cstart // rb
    last_blk = jnp.maximum(cend - 1, 0) // rb
    nvis = jnp.where(counts > 0, last_blk - first_blk + 1, 0)
    vend = jnp.cumsum(nvis)
    vstart = vend - nvis
    total = vend[-1]
    vid = jnp.arange(n_vis, dtype=I32)
    vc = jnp.minimum(vid, total - 1)
    c = jnp.searchsorted(vend, vc, side="right").astype(I32)
    blk = first_blk[c] + (vc - vstart[c])
    lo = jnp.maximum(cstart[c], blk * rb) - blk * rb
    hi = jnp.minimum(cend[c], (blk + 1) * rb) - blk * rb
    valid = vid < total
    prev_blk = jnp.concatenate([jnp.full((1,), -1, I32), blk[:-1]])
    next_blk = jnp.concatenate([blk[1:], jnp.full((1,), -1, I32)])
    first = valid & (blk != prev_blk)
    last = valid & ((blk != next_blk) | (vid == total - 1))
    fin = vid == total - 1
    has_next = blk < n_blk - 1
    flags = (valid * V_VALID + first * V_FIRST + last * V_LAST + fin * V_FIN + has_next * V_NEXT).astype(I32)
    pair_lo, pair_hi = _pair_tables()
    grp = c // PAIRS_PER_GROUP
    pidx = c % PAIRS_PER_GROUP
    plan = dict(blk=blk.astype(I32), ea=(grp * EXPERTS_PER_GROUP + pair_lo[pidx]).astype(I32),
                eb=(grp * EXPERTS_PER_GROUP + pair_hi[pidx]).astype(I32),
                lo=lo.astype(I32), hi=hi.astype(I32), flags=flags,
                row_tok=order.reshape(n_blk, 1, rb))
    return plan, order


def _rope_tables(n):
    inv_freq = ROPE_THETA ** (-jnp.arange(0, QK_DIM, 2, dtype=F32) / QK_DIM)
    ang = jnp.arange(n, dtype=F32)[:, None] * inv_freq[None, :]
    cos, sin = jnp.cos(ang), jnp.sin(ang)
    cos_t = jnp.tile(cos, (1, 2 * HEAD_W // QK_DIM))
    sin_t = jnp.tile(jnp.concatenate([-sin, sin], axis=1), (1, HEAD_W // QK_DIM))
    return cos_t, sin_t


def _trunk(x, p, lam_init):
    b, n, d = x.shape
    t = b * n
    assert n % TOKEN_TILE == 0 and n % Q_TILE == 0 and n % KV_CHUNK == 0 and t % ROW_BLOCK == 0
    x2 = x.reshape(t, d)
    cos, sin = _rope_tables(n)
    zp, q, k, vt = _inproj(x2, p["g_mix"], p["w_in"], p["w_vt"], cos, sin, n)
    attn = _attention(q.reshape(b, n, ATTN_WIDTH), k.reshape(b, n, ATTN_WIDTH), vt,
                      p["lam_params"], p["subln_g"], lam_init)
    h, cls3, gates3 = _mix(x2, zp, attn.reshape(t, ATTN_WIDTH), p["w_out"], p["w_pool"], p["pool_scale"],
                           p["g_ffn"], p["wr_hi"], p["wr_lo"], p["r_bias"], n)
    cls = cls3.reshape(t)
    gates = jnp.transpose(gates3, (1, 0, 2)).reshape(2, t)
    plan, order = _dispatch_plan(cls, t)
    gs = gates[:, order]
    gates_rows = jnp.concatenate([jnp.broadcast_to(gs[0][:, None], (t, LANES // 2)),
                                  jnp.broadcast_to(gs[1][:, None], (t, LANES // 2))], axis=1)
    y = _moe(plan, h, gates_rows, p["g_ffn"], p["g_final"], p["w_gate"], p["w_up"], p["w_down"])
    return y.reshape(b, n, d)


def kernel(x_prompt, x_sample, g_mix, w_in, w_pool, pool_scale, lambda_q1, lambda_k1, lambda_q2, lambda_k2,
           subln_g, w_out, g_ffn, w_router_group, b_router_group, w_router_expert, b_router_expert,
           w_gate, w_up, w_down, g_final):
    depth = g_mix.shape[0]
    assert depth == 1
    l = 0
    lam_init = 0.8 - 0.6 * 1.0
    d = g_mix.shape[1]
    wr = jnp.zeros((ROUTER_ROWS, d), F32)
    wr = wr.at[0:MOE_GROUPS].set(w_router_group[l].T)
    wr = wr.at[SUBLANES:SUBLANES + MOE_GROUPS * EXPERTS_PER_GROUP].set(w_router_expert[l].T)
    wr_hi = wr.astype(BF16)
    wr_lo = (wr - wr_hi.astype(F32)).astype(BF16)
    r_bias = jnp.zeros((ROUTER_ROWS, 1), F32)
    r_bias = r_bias.at[0:MOE_GROUPS, 0].set(b_router_group[l])
    r_bias = r_bias.at[SUBLANES:SUBLANES + MOE_GROUPS * EXPERTS_PER_GROUP, 0].set(b_router_expert[l])
    p = dict(
        g_mix=g_mix[l][None, :], w_in=w_in[l].astype(BF16),
        w_vt=w_in[l][:, POOL_WIDTH + 2 * ATTN_WIDTH:].T.astype(BF16), w_pool=w_pool[l].astype(BF16),
        pool_scale=pool_scale[l][None, :],
        lam_params=jnp.stack([lambda_q1[l], lambda_k1[l], lambda_q2[l], lambda_k2[l]]),
        subln_g=subln_g[l][:, None], w_out=w_out[l].astype(BF16), g_ffn=g_ffn[l][None, :],
        wr_hi=wr_hi, wr_lo=wr_lo, r_bias=r_bias,
        w_gate=w_gate[l].astype(BF16), w_up=w_up[l].astype(BF16), w_down=w_down[l].astype(BF16),
        g_final=g_final[None, :],
    )
    return (_trunk(x_prompt, p, lam_init), _trunk(x_sample, p, lam_init))
```

```python
import functools

import jax
import jax.numpy as jnp
from jax import lax
from jax.experimental import pallas as pl
from jax.experimental.pallas import tpu as pltpu

F32, BF16, I32 = jnp.float32, jnp.bfloat16, jnp.int32

EPS = 1e-6
ROPE_THETA = 10000.0
LOG2_E = 1.4426950408889634
POOL_WINDOWS = (2, 4, 8, 16)
POOL_GROUP_DIM = 128
POOL_WIDTH = POOL_GROUP_DIM * len(POOL_WINDOWS)
HEADS = 4
QK_DIM = 64
HEAD_W = 2 * QK_DIM
ATTN_WIDTH = HEADS * HEAD_W
MOE_GROUPS = 4
EXPERTS_PER_GROUP = 8
PAIRS_PER_GROUP = EXPERTS_PER_GROUP * (EXPERTS_PER_GROUP - 1) // 2
N_CLASSES = MOE_GROUPS * PAIRS_PER_GROUP
ROUTER_ROWS = 48

SUBLANES = 8
LANES = 128
MXU_N = 256
VMEM_LIMIT = 48 * 1024 * 1024

TOKEN_TILE = 512
Q_TILE = 512
KV_CHUNK = 1024
SCORE_LOOKAHEAD = 2
ROW_BLOCK = 128

V_VALID, V_FIRST, V_LAST, V_FIN = 1, 2, 4, 8

_NT = (((1,), (1,)), ((), ()))


def _rms(x, g):
    return x * lax.rsqrt(jnp.mean(x * x, axis=-1, keepdims=True) + EPS) * g


def _inproj_kernel(x_ref, g_ref, w_ref, wvt_ref, cos_ref, sin_ref, zp_ref, q_ref, k_ref, vt_ref):
    xb = _rms(x_ref[...], g_ref[...]).astype(BF16)
    zp_ref[...] = jnp.dot(xb, w_ref[:, 0:POOL_WIDTH], preferred_element_type=F32)
    q0, k0 = POOL_WIDTH, POOL_WIDTH + ATTN_WIDTH
    vt_ref[0] = lax.dot_general(wvt_ref[...], xb, _NT, preferred_element_type=F32).astype(BF16)
    cos, sin = cos_ref[...], sin_ref[...]
    lane = lax.broadcasted_iota(I32, cos.shape, 1)
    lower = (lane % QK_DIM) < (QK_DIM // 2)
    half = QK_DIM // 2

    def rope(z):
        partner = jnp.where(lower, pltpu.roll(z, HEAD_W - half, 1), pltpu.roll(z, half, 1))
        return z * cos + partner * sin

    zq = jnp.dot(xb, w_ref[:, q0:q0 + ATTN_WIDTH], preferred_element_type=F32)
    zk = jnp.dot(xb, w_ref[:, k0:k0 + ATTN_WIDTH], preferred_element_type=F32)
    scale = QK_DIM ** -0.5 * LOG2_E
    for h in range(HEADS):
        sl = slice(h * HEAD_W, (h + 1) * HEAD_W)
        q_ref[:, sl] = (rope(zq[:, sl]) * scale).astype(BF16)
        k_ref[:, sl] = rope(zk[:, sl]).astype(BF16)


def _inproj(x2, g_mix, w_in, w_vt, cos, sin, seq):
    t, d = x2.shape
    tt = TOKEN_TILE
    per_seq = seq // tt
    row = lambda i: (i, 0)
    return pl.pallas_call(
        _inproj_kernel,
        grid=(t // tt,),
        in_specs=[
            pl.BlockSpec((tt, d), row),
            pl.BlockSpec((1, d), lambda i: (0, 0)),
            pl.BlockSpec(w_in.shape, lambda i: (0, 0)),
            pl.BlockSpec(w_vt.shape, lambda i: (0, 0)),
            pl.BlockSpec((tt, HEAD_W), lambda i: (i % per_seq, 0)),
            pl.BlockSpec((tt, HEAD_W), lambda i: (i % per_seq, 0)),
        ],
        out_specs=[
            pl.BlockSpec((tt, POOL_WIDTH), row),
            pl.BlockSpec((tt, ATTN_WIDTH), row),
            pl.BlockSpec((tt, ATTN_WIDTH), row),
            pl.BlockSpec((1, ATTN_WIDTH, tt), lambda i: (i // per_seq, 0, i % per_seq)),
        ],
        out_shape=[
            jax.ShapeDtypeStruct((t, POOL_WIDTH), F32),
            jax.ShapeDtypeStruct((t, ATTN_WIDTH), BF16),
            jax.ShapeDtypeStruct((t, ATTN_WIDTH), BF16),
            jax.ShapeDtypeStruct((t // seq, ATTN_WIDTH, seq), BF16),
        ],
        compiler_params=pltpu.CompilerParams(
            dimension_semantics=("arbitrary",), vmem_limit_bytes=VMEM_LIMIT),
        name="inproj",
    )(x2, g_mix, w_in, w_vt, cos, sin)


def _attn_kernel(lam_ref, q_ref, k_ref, vt_ref, sg_ref, o_ref, *, n_chunks, lam_init):
    tq = q_ref.shape[1]
    lp = lam_ref[...]
    lam = (jnp.exp(jnp.sum(lp[0:1] * lp[1:2], axis=-1, keepdims=True))
           - jnp.exp(jnp.sum(lp[2:3] * lp[3:4], axis=-1, keepdims=True)) + lam_init)
    q = q_ref[0]
    lane = lax.broadcasted_iota(I32, q.shape, 1)
    zero = jnp.zeros_like(q)
    q2 = jnp.concatenate([jnp.where(lane < QK_DIM, q, zero), jnp.where(lane >= QK_DIM, q, zero)], axis=0)

    n_tiles = 2 * tq // MXU_N
    groups = MXU_N // LANES
    m = [jnp.full((1, LANES), -jnp.inf, F32) for _ in range(n_tiles * groups)]
    lsum = [jnp.zeros((SUBLANES, LANES), F32) for _ in range(n_tiles * groups)]
    acc = [jnp.zeros((HEAD_W, MXU_N), F32) for _ in range(n_tiles)]

    def scores(c):
        kc = k_ref[0, c * KV_CHUNK:(c + 1) * KV_CHUNK, :]
        return [lax.dot_general(kc, q2[t * MXU_N:(t + 1) * MXU_N], _NT, preferred_element_type=F32)
                for t in range(n_tiles)]

    pending = [scores(c) for c in range(min(SCORE_LOOKAHEAD, n_chunks))]
    for c in range(n_chunks):
        st = pending.pop(0)
        if c + SCORE_LOOKAHEAD < n_chunks:
            pending.append(scores(c + SCORE_LOOKAHEAD))
        vtc = vt_ref[0, :, c * KV_CHUNK:(c + 1) * KV_CHUNK]
        for t in range(n_tiles):
            ps, alphas = [], []
            for g in range(groups):
                j = t * groups + g
                s = st[t][:, g * LANES:(g + 1) * LANES]
                m_new = jnp.maximum(m[j], jnp.max(s, axis=0, keepdims=True))
                alpha = jnp.exp2(m[j] - m_new)
                p = jnp.exp2(s - m_new)
                lsum[j] = alpha * lsum[j] + jnp.sum(p.reshape(KV_CHUNK // SUBLANES, SUBLANES, LANES), axis=0)
                m[j] = m_new
                ps.append(p.astype(BF16))
                alphas.append(alpha)
            acc[t] = (jnp.concatenate(alphas, axis=1) * acc[t]
                      + jnp.dot(vtc, jnp.concatenate(ps, axis=1), preferred_element_type=F32))
    acc = jnp.concatenate(acc, axis=1)
    lsum = jnp.concatenate(lsum, axis=1)
    o = acc / jnp.sum(lsum, axis=0, keepdims=True)
    od = o[:, :tq] - lam * o[:, tq:]
    y = od * lax.rsqrt(jnp.mean(od * od, axis=0, keepdims=True) + EPS) * sg_ref[...] * (1.0 - lam_init)
    o_ref[0] = y.T.astype(BF16)


def _attention(q, k, vt, lam_params, subln_g_col, lam_init):
    b, n, _ = q.shape
    kernel = functools.partial(_attn_kernel, n_chunks=n // KV_CHUNK, lam_init=lam_init)
    q_spec = pl.BlockSpec((1, Q_TILE, HEAD_W), lambda bi, h, i: (bi, i, h))
    return pl.pallas_call(
        kernel,
        grid=(b, HEADS, n // Q_TILE),
        in_specs=[
            pl.BlockSpec(lam_params.shape, lambda bi, h, i: (0, 0)),
            q_spec,
            pl.BlockSpec((1, n, HEAD_W), lambda bi, h, i: (bi, 0, h)),
            pl.BlockSpec((1, HEAD_W, n), lambda bi, h, i: (bi, h, 0)),
            pl.BlockSpec((HEAD_W, 1), lambda bi, h, i: (0, 0)),
        ],
        out_specs=q_spec,
        out_shape=jax.ShapeDtypeStruct((b, n, ATTN_WIDTH), BF16),
        compiler_params=pltpu.CompilerParams(
            dimension_semantics=("arbitrary", "arbitrary", "arbitrary"), vmem_limit_bytes=VMEM_LIMIT),
        name="diff_attention",
    )(lam_params, q, k, vt, subln_g_col)


def _first_argmax(vals, best, rows):
    return jnp.min(jnp.where(vals == best, rows, SUBLANES), axis=0, keepdims=True)


def _mix_kernel(x_ref, zp_ref, zprev_ref, znext_ref, at_ref, wout_ref, wpool_ref, ps_ref, gf_ref,
                wrh_ref, wrl_ref, rb_ref, h_ref, cls_ref, gate_ref, *, seq):
    tt = x_ref.shape[0]
    pos0 = (pl.program_id(0) * tt) % seq
    halo = SUBLANES
    zprev = jnp.where(pos0 == 0, 0.0, zprev_ref[...])
    znext = jnp.where(pos0 + tt == seq, 0.0, znext_ref[...])
    zext = jnp.concatenate([zprev, zp_ref[...], znext], axis=0)
    pos = pos0 + lax.broadcasted_iota(I32, (tt, POOL_GROUP_DIM), 0)
    pooled = []
    for g, w in enumerate(POOL_WINDOWS):
        hf = w // 2
        zg = zext[:, g * POOL_GROUP_DIM:(g + 1) * POOL_GROUP_DIM]
        win = zg[halo - hf:halo - hf + tt]
        for s in range(-hf + 1, hf):
            win = win + zg[halo + s:halo + s + tt]
        cnt = (jnp.minimum(pos + hf, seq) - jnp.maximum(pos - hf, 0)).astype(F32)
        pl_g = (win / cnt - zg[halo:halo + tt]).astype(BF16)
        po = jnp.dot(pl_g, wpool_ref[g], preferred_element_type=F32)
        pooled.append((po * ps_ref[:, g * POOL_GROUP_DIM:(g + 1) * POOL_GROUP_DIM]).astype(BF16))
    mix = jnp.concatenate(pooled + [at_ref[...]], axis=1)
    h = x_ref[...] + jnp.dot(mix, wout_ref[...], preferred_element_type=F32)
    h_ref[...] = h

    hn = _rms(h, gf_ref[...])
    hi = hn.astype(BF16)
    lo = (hn - hi.astype(F32)).astype(BF16)
    wrh, wrl = wrh_ref[...], wrl_ref[...]
    lt = (lax.dot_general(wrh, hi, _NT, preferred_element_type=F32)
          + lax.dot_general(wrl, hi, _NT, preferred_element_type=F32)
          + lax.dot_general(wrh, lo, _NT, preferred_element_type=F32)) + rb_ref[...]
    rows = lax.broadcasted_iota(I32, (SUBLANES, tt), 0)
    lg = jnp.where(rows < MOE_GROUPS, lt[0:SUBLANES], -jnp.inf)
    eg = jnp.exp(lg - jnp.max(lg, axis=0, keepdims=True))
    pg = eg / jnp.sum(eg, axis=0, keepdims=True)
    g_top = jnp.max(pg, axis=0, keepdims=True)
    g_idx = _first_argmax(pg, g_top, rows)
    le = lt[SUBLANES:2 * SUBLANES]
    for g in range(1, MOE_GROUPS):
        le = jnp.where(g_idx == g, lt[SUBLANES * (g + 1):SUBLANES * (g + 2)], le)
    ee = jnp.exp(le - jnp.max(le, axis=0, keepdims=True))
    pe = ee / jnp.sum(ee, axis=0, keepdims=True)
    e1 = jnp.max(pe, axis=0, keepdims=True)
    i1 = _first_argmax(pe, e1, rows)
    pe2 = jnp.where(rows == i1, -1.0, pe)
    e2 = jnp.max(pe2, axis=0, keepdims=True)
    i2 = _first_argmax(pe2, e2, rows)
    den = e1 + e2
    gate1 = g_top * (e1 / den)
    gate2 = g_top * (e2 / den)
    a = jnp.minimum(i1, i2)
    b = jnp.maximum(i1, i2)
    pair = ((a * (2 * EXPERTS_PER_GROUP - 1 - a)) >> 1) + (b - a - 1)
    cls_ref[0] = g_idx * PAIRS_PER_GROUP + pair
    first_lower = i1 < i2
    gate_ref[0] = jnp.concatenate(
        [jnp.where(first_lower, gate1, gate2), jnp.where(first_lower, gate2, gate1)], axis=0)


def _mix(x2, zp, attn, w_out, w_pool, pool_scale, g_ffn, wr_hi, wr_lo, r_bias, seq):
    t, d = x2.shape
    tt = TOKEN_TILE
    nt = t // tt
    hb = tt // SUBLANES
    last_halo = t // SUBLANES - 1
    row = lambda i: (i, 0)
    const2 = lambda i: (0, 0)
    kernel = functools.partial(_mix_kernel, seq=seq)
    return pl.pallas_call(
        kernel,
        grid=(nt,),
        in_specs=[
            pl.BlockSpec((tt, d), row),
            pl.BlockSpec((tt, POOL_WIDTH), row),
            pl.BlockSpec((SUBLANES, POOL_WIDTH), lambda i: (jnp.maximum(i * hb - 1, 0), 0)),
            pl.BlockSpec((SUBLANES, POOL_WIDTH), lambda i: (jnp.minimum((i + 1) * hb, last_halo), 0)),
            pl.BlockSpec((tt, ATTN_WIDTH), row),
            pl.BlockSpec(w_out.shape, const2),
            pl.BlockSpec(w_pool.shape, lambda i: (0, 0, 0)),
            pl.BlockSpec((1, POOL_WIDTH), const2),
            pl.BlockSpec((1, d), const2),
            pl.BlockSpec(wr_hi.shape, const2),
            pl.BlockSpec(wr_lo.shape, const2),
            pl.BlockSpec(r_bias.shape, const2),
        ],
        out_specs=[
            pl.BlockSpec((tt, d), row),
            pl.BlockSpec((1, 1, tt), lambda i: (i, 0, 0)),
            pl.BlockSpec((1, 2, tt), lambda i: (i, 0, 0)),
        ],
        out_shape=[
            jax.ShapeDtypeStruct((t, d), F32),
            jax.ShapeDtypeStruct((nt, 1, tt), I32),
            jax.ShapeDtypeStruct((nt, 2, tt), F32),
        ],
        compiler_params=pltpu.CompilerParams(
            dimension_semantics=("arbitrary",), vmem_limit_bytes=VMEM_LIMIT),
        name="mix_router",
    )(x2, zp, zp, zp, attn, w_out, w_pool, pool_scale, g_ffn, wr_hi, wr_lo, r_bias)


def _moe_kernel(vblk_ref, vea_ref, veb_ref, vlo_ref, vhi_ref, vfl_ref,
                tokc_ref, tokn_ref, tokp_ref, gates_ref, gf_ref, gfin_ref,
                wga_ref, wua_ref, wda_ref, wgb_ref, wub_ref, wdb_ref, h_hbm,
                out_hbm, hbuf, xnbuf, acc, obuf, gsem, ssem):
    v = pl.program_id(0)
    flags = vfl_ref[v]
    rb = hbuf.shape[1]
    blk = vblk_ref[v]
    slot = blk % 2
    other = 1 - slot
    valid = (flags & V_VALID) != 0
    first = (flags & V_FIRST) != 0
    last = (flags & V_LAST) != 0

    def row_in(tok_ref, r, s):
        return pltpu.make_async_copy(h_hbm.at[pl.ds(tok_ref[0, 0, r], 1), :],
                                     hbuf.at[s, pl.ds(r, 1), :], gsem.at[s])

    def row_out(tok_ref, r, s):
        return pltpu.make_async_copy(obuf.at[s, pl.ds(r, 1), :],
                                     out_hbm.at[pl.ds(tok_ref[0, 0, r], 1), :], ssem.at[s])

    def rolled(fn):
        def body(r, c):
            fn(r)
            return c
        lax.fori_loop(0, rb, body, 0, unroll=8)

    def wait_gather(s):
        rolled(lambda r: row_in(tokc_ref, r, s).wait())

    def wait_scatter(s):
        rolled(lambda r: row_out(tokc_ref, r, s).wait())

    def experts(xn):
        rows = lax.broadcasted_iota(I32, (rb, 1), 0)
        seg = (rows >= vlo_ref[v]) & (rows < vhi_ref[v])
        gts = gates_ref[...]

        def hidden(wg_ref, wu_ref, gate):
            hid = (jax.nn.silu(jnp.dot(xn, wg_ref[0], preferred_element_type=F32))
                   * jnp.dot(xn, wu_ref[0], preferred_element_type=F32))
            return jnp.where(seg, hid * gate, 0.0).astype(BF16)

        ha = hidden(wga_ref, wua_ref, gts[:, 0:1])
        hb = hidden(wgb_ref, wub_ref, gts[:, 1:2])
        return (jnp.dot(ha, wda_ref[0], preferred_element_type=F32)
                + jnp.dot(hb, wdb_ref[0], preferred_element_type=F32))

    def finalize():
        y = _rms(hbuf[slot] + acc[...], gfin_ref[...])

        @pl.when(blk >= 2)
        def _():
            wait_scatter(slot)
        obuf[slot] = y

    @pl.when(valid & first & (v > 0))
    def _first_visit():
        wait_gather(slot)
        xn = _rms(hbuf[slot], gf_ref[...]).astype(BF16)
        xnbuf[...] = xn
        for r in range(rb):
            row_in(tokn_ref, r, other).start(priority=r % 2)
        for r in range(rb):
            row_out(tokp_ref, r, other).start(priority=r % 2)
        acc[...] = experts(xn)

        @pl.when(last)
        def _():
            finalize()

    @pl.when(valid & first & (v == 0))
    def _very_first_visit():
        rolled(lambda r: row_in(tokc_ref, r, slot).start())
        wait_gather(slot)
        rolled(lambda r: row_in(tokn_ref, r, other).start())
        xn = _rms(hbuf[slot], gf_ref[...]).astype(BF16)
        xnbuf[...] = xn
        acc[...] = experts(xn)

        @pl.when(last)
        def _():
            finalize()

    @pl.when(valid & jnp.logical_not(first))
    def _continued_visit():
        acc[...] += experts(xnbuf[...])

        @pl.when(last)
        def _():
            finalize()

    @pl.when((flags & V_FIN) != 0)
    def _final_visit():
        rolled(lambda r: row_out(tokc_ref, r, slot).start())
        wait_scatter(slot)
        wait_scatter(other)
        wait_gather(other)


def _moe(plan, h, gates_rows, g_ffn, g_final, w_gate, w_up, w_down):
    t, d = h.shape
    rb = ROW_BLOCK
    n_blk = t // rb
    n_vis = plan["blk"].shape[0]
    dff = w_gate.shape[2]
    cur = lambda v, blk, *_: (blk[v], 0)
    tok_cur = lambda v, blk, *_: (blk[v], 0, 0)
    tok_nxt = lambda v, blk, *_: (jnp.minimum(blk[v] + 1, n_blk - 1), 0, 0)
    tok_prv = lambda v, blk, *_: (jnp.maximum(blk[v] - 1, 0), 0, 0)
    const2 = lambda v, *_: (0, 0)
    wa = lambda v, blk, ea, *_: (ea[v], 0, 0)
    wb = lambda v, blk, ea, eb, *_: (eb[v], 0, 0)
    grid_spec = pltpu.PrefetchScalarGridSpec(
        num_scalar_prefetch=6,
        grid=(n_vis,),
        in_specs=[
            pl.BlockSpec((1, 1, rb), tok_cur, memory_space=pltpu.SMEM),
            pl.BlockSpec((1, 1, rb), tok_nxt, memory_space=pltpu.SMEM),
            pl.BlockSpec((1, 1, rb), tok_prv, memory_space=pltpu.SMEM),
            pl.BlockSpec((rb, 2), cur),
            pl.BlockSpec((1, d), const2),
            pl.BlockSpec((1, d), const2),
            pl.BlockSpec((1, d, dff), wa),
            pl.BlockSpec((1, d, dff), wa),
            pl.BlockSpec((1, dff, d), wa),
            pl.BlockSpec((1, d, dff), wb),
            pl.BlockSpec((1, d, dff), wb),
            pl.BlockSpec((1, dff, d), wb),
            pl.BlockSpec(memory_space=pl.ANY),
        ],
        out_specs=pl.BlockSpec(memory_space=pl.ANY),
        scratch_shapes=[
            pltpu.VMEM((2, rb, d), F32),
            pltpu.VMEM((rb, d), BF16),
            pltpu.VMEM((rb, d), F32),
            pltpu.VMEM((2, rb, d), F32),
            pltpu.SemaphoreType.DMA((2,)),
            pltpu.SemaphoreType.DMA((2,)),
        ],
    )
    return pl.pallas_call(
        _moe_kernel,
        grid_spec=grid_spec,
        out_shape=jax.ShapeDtypeStruct((t, d), F32),
        compiler_params=pltpu.CompilerParams(
            dimension_semantics=("arbitrary",), vmem_limit_bytes=VMEM_LIMIT),
        name="moe",
    )(plan["blk"], plan["ea"], plan["eb"], plan["lo"], plan["hi"], plan["flags"],
      plan["row_tok"], plan["row_tok"], plan["row_tok"], gates_rows, g_ffn, g_final,
      w_gate, w_up, w_down, w_gate, w_up, w_down, h)


def _pair_tables():
    lo, hi = [], []
    for a in range(EXPERTS_PER_GROUP):
        for b in range(a + 1, EXPERTS_PER_GROUP):
            lo.append(a)
            hi.append(b)
    return jnp.array(lo, I32), jnp.array(hi, I32)


def _dispatch_plan(cls, t):
    rb = ROW_BLOCK
    n_blk = t // rb
    n_vis = n_blk + N_CLASSES - 1
    order = jnp.argsort(cls, stable=True).astype(I32)
    counts = jnp.bincount(cls, length=N_CLASSES).astype(I32)
    cend = jnp.cumsum(counts)
    cstart = cend - counts
    first_blk = cstart // rb
    last_blk = jnp.maximum(cend - 1, 0) // rb
    nvis = jnp.where(counts > 0, last_blk - first_blk + 1, 0)
    vend = jnp.cumsum(nvis)
    vstart = vend - nvis
    total = vend[-1]
    vid = jnp.arange(n_vis, dtype=I32)
    vc = jnp.minimum(vid, total - 1)
    c = jnp.searchsorted(vend, vc, side="right").astype(I32)
    blk = first_blk[c] + (vc - vstart[c])
    lo = jnp.maximum(cstart[c], blk * rb) - blk * rb
    hi = jnp.minimum(cend[c], (blk + 1) * rb) - blk * rb
    valid = vid < total
    prev_blk = jnp.concatenate([jnp.full((1,), -1, I32), blk[:-1]])
    next_blk = jnp.concatenate([blk[1:], jnp.full((1,), -1, I32)])
    first = valid & (blk != prev_blk)
    last = valid & ((blk != next_blk) | (vid == total - 1))
    fin = vid == total - 1
    flags = (valid * V_VALID + first * V_FIRST + last * V_LAST + fin * V_FIN).astype(I32)
    pair_lo, pair_hi = _pair_tables()
    grp = c // PAIRS_PER_GROUP
    pidx = c % PAIRS_PER_GROUP
    plan = dict(blk=blk.astype(I32), ea=(grp * EXPERTS_PER_GROUP + pair_lo[pidx]).astype(I32),
                eb=(grp * EXPERTS_PER_GROUP + pair_hi[pidx]).astype(I32),
                lo=lo.astype(I32), hi=hi.astype(I32), flags=flags,
                row_tok=order.reshape(n_blk, 1, rb))
    return plan, order


def _rope_tables(n):
    inv_freq = ROPE_THETA ** (-jnp.arange(0, QK_DIM, 2, dtype=F32) / QK_DIM)
    ang = jnp.arange(n, dtype=F32)[:, None] * inv_freq[None, :]
    cos, sin = jnp.cos(ang), jnp.sin(ang)
    cos_t = jnp.tile(cos, (1, 2 * HEAD_W // QK_DIM))
    sin_t = jnp.tile(jnp.concatenate([-sin, sin], axis=1), (1, HEAD_W // QK_DIM))
    return cos_t, sin_t


def _trunk(x, p, lam_init):
    b, n, d = x.shape
    t = b * n
    assert n % TOKEN_TILE == 0 and n % Q_TILE == 0 and n % KV_CHUNK == 0
    assert t % ROW_BLOCK == 0 and t // ROW_BLOCK >= 2
    x2 = x.reshape(t, d)
    cos, sin = _rope_tables(n)
    zp, q, k, vt = _inproj(x2, p["g_mix"], p["w_in"], p["w_vt"], cos, sin, n)
    attn = _attention(q.reshape(b, n, ATTN_WIDTH), k.reshape(b, n, ATTN_WIDTH), vt,
                      p["lam_params"], p["subln_g"], lam_init)
    h, cls3, gates3 = _mix(x2, zp, attn.reshape(t, ATTN_WIDTH), p["w_out"], p["w_pool"], p["pool_scale"],
                           p["g_ffn"], p["wr_hi"], p["wr_lo"], p["r_bias"], n)
    cls = cls3.reshape(t)
    gates = jnp.transpose(gates3, (1, 0, 2)).reshape(2, t)
    plan, order = _dispatch_plan(cls, t)
    gates_rows = gates[:, order].T
    y = _moe(plan, h, gates_rows, p["g_ffn"], p["g_final"], p["w_gate"], p["w_up"], p["w_down"])
    return y.reshape(b, n, d)


def kernel(x_prompt, x_sample, g_mix, w_in, w_pool, pool_scale, lambda_q1, lambda_k1, lambda_q2, lambda_k2,
           subln_g, w_out, g_ffn, w_router_group, b_router_group, w_router_expert, b_router_expert,
           w_gate, w_up, w_down, g_final):
    depth = g_mix.shape[0]
    assert depth == 1
    l = 0
    lam_init = 0.8 - 0.6 * 1.0
    d = g_mix.shape[1]
    wr = jnp.zeros((ROUTER_ROWS, d), F32)
    wr = wr.at[0:MOE_GROUPS].set(w_router_group[l].T)
    wr = wr.at[SUBLANES:SUBLANES + MOE_GROUPS * EXPERTS_PER_GROUP].set(w_router_expert[l].T)
    wr_hi = wr.astype(BF16)
    wr_lo = (wr - wr_hi.astype(F32)).astype(BF16)
    r_bias = jnp.zeros((ROUTER_ROWS, 1), F32)
    r_bias = r_bias.at[0:MOE_GROUPS, 0].set(b_router_group[l])
    r_bias = r_bias.at[SUBLANES:SUBLANES + MOE_GROUPS * EXPERTS_PER_GROUP, 0].set(b_router_expert[l])
    p = dict(
        g_mix=g_mix[l][None, :], w_in=w_in[l].astype(BF16),
        w_vt=w_in[l][:, POOL_WIDTH + 2 * ATTN_WIDTH:].T.astype(BF16), w_pool=w_pool[l].astype(BF16),
        pool_scale=pool_scale[l][None, :],
        lam_params=jnp.stack([lambda_q1[l], lambda_k1[l], lambda_q2[l], lambda_k2[l]]),
        subln_g=subln_g[l][:, None], w_out=w_out[l].astype(BF16), g_ffn=g_ffn[l][None, :],
        wr_hi=wr_hi, wr_lo=wr_lo, r_bias=r_bias,
        w_gate=w_gate[l].astype(BF16), w_up=w_up[l].astype(BF16), w_down=w_down[l].astype(BF16),
        g_final=g_final[None, :],
    )
    return (_trunk(x_prompt, p, lam_init), _trunk(x_sample, p, lam_init))
```

```python
import functools

import jax
import jax.numpy as jnp
from jax import lax
from jax.experimental import pallas as pl
from jax.experimental.pallas import tpu as pltpu

F32, BF16, I32 = jnp.float32, jnp.bfloat16, jnp.int32

EPS = 1e-6
ROPE_THETA = 10000.0
LOG2_E = 1.4426950408889634
POOL_WINDOWS = (2, 4, 8, 16)
POOL_GROUP_DIM = 128
POOL_WIDTH = POOL_GROUP_DIM * len(POOL_WINDOWS)
HEADS = 4
QK_DIM = 64
HEAD_W = 2 * QK_DIM
ATTN_WIDTH = HEADS * HEAD_W
MOE_GROUPS = 4
EXPERTS_PER_GROUP = 8
PAIRS_PER_GROUP = EXPERTS_PER_GROUP * (EXPERTS_PER_GROUP - 1) // 2
N_CLASSES = MOE_GROUPS * PAIRS_PER_GROUP
ROUTER_ROWS = 48

SUBLANES = 8
LANES = 128
MXU_N = 256
VMEM_LIMIT = 48 * 1024 * 1024

TOKEN_TILE = 512
MIX_SPLIT = 2
Q_TILE_LONG, Q_TILE_SHORT = 256, 512
LONG_SEQ = 4096
KV_CHUNK = 1024
SCORE_LOOKAHEAD = 2
ROW_BLOCK = 256

V_VALID, V_FIRST, V_LAST, V_FIN = 1, 2, 4, 8

_NT = (((1,), (1,)), ((), ()))


def _rms(x, g):
    return x * lax.rsqrt(jnp.mean(x * x, axis=-1, keepdims=True) + EPS) * g


def _inproj_kernel(x_ref, g_ref, w_ref, wvt_ref, cos_ref, sin_ref, zp_ref, q_ref, k_ref, vt_ref):
    xb = _rms(x_ref[...], g_ref[...]).astype(BF16)
    zp_ref[...] = jnp.dot(xb, w_ref[:, 0:POOL_WIDTH], preferred_element_type=F32)
    q0, k0 = POOL_WIDTH, POOL_WIDTH + ATTN_WIDTH
    vt_ref[0] = lax.dot_general(wvt_ref[...], xb, _NT, preferred_element_type=F32).astype(BF16)
    cos, sin = cos_ref[...], sin_ref[...]
    lane = lax.broadcasted_iota(I32, cos.shape, 1)
    lower = (lane % QK_DIM) < (QK_DIM // 2)
    half = QK_DIM // 2

    def rope(z):
        partner = jnp.where(lower, pltpu.roll(z, HEAD_W - half, 1), pltpu.roll(z, half, 1))
        return z * cos + partner * sin

    zq = jnp.dot(xb, w_ref[:, q0:q0 + ATTN_WIDTH], preferred_element_type=F32)
    zk = jnp.dot(xb, w_ref[:, k0:k0 + ATTN_WIDTH], preferred_element_type=F32)
    scale = QK_DIM ** -0.5 * LOG2_E
    for h in range(HEADS):
        sl = slice(h * HEAD_W, (h + 1) * HEAD_W)
        q_ref[:, sl] = (rope(zq[:, sl]) * scale).astype(BF16)
        k_ref[:, sl] = rope(zk[:, sl]).astype(BF16)


def _inproj(x2, g_mix, w_in, w_vt, cos, sin, seq):
    t, d = x2.shape
    tt = TOKEN_TILE
    per_seq = seq // tt
    row = lambda i: (i, 0)
    return pl.pallas_call(
        _inproj_kernel,
        grid=(t // tt,),
        in_specs=[
            pl.BlockSpec((tt, d), row),
            pl.BlockSpec((1, d), lambda i: (0, 0)),
            pl.BlockSpec(w_in.shape, lambda i: (0, 0)),
            pl.BlockSpec(w_vt.shape, lambda i: (0, 0)),
            pl.BlockSpec((tt, HEAD_W), lambda i: (i % per_seq, 0)),
            pl.BlockSpec((tt, HEAD_W), lambda i: (i % per_seq, 0)),
        ],
        out_specs=[
            pl.BlockSpec((tt, POOL_WIDTH), row),
            pl.BlockSpec((tt, ATTN_WIDTH), row),
            pl.BlockSpec((tt, ATTN_WIDTH), row),
            pl.BlockSpec((1, ATTN_WIDTH, tt), lambda i: (i // per_seq, 0, i % per_seq)),
        ],
        out_shape=[
            jax.ShapeDtypeStruct((t, POOL_WIDTH), F32),
            jax.ShapeDtypeStruct((t, ATTN_WIDTH), BF16),
            jax.ShapeDtypeStruct((t, ATTN_WIDTH), BF16),
            jax.ShapeDtypeStruct((t // seq, ATTN_WIDTH, seq), BF16),
        ],
        compiler_params=pltpu.CompilerParams(
            dimension_semantics=("arbitrary",), vmem_limit_bytes=VMEM_LIMIT),
        name="inproj",
    )(x2, g_mix, w_in, w_vt, cos, sin)


def _attn_kernel(lam_ref, q_ref, k_ref, vt_ref, sg_ref, o_ref, *, n_chunks, lam_init):
    tq = q_ref.shape[1]
    lp = lam_ref[...]
    lam = (jnp.exp(jnp.sum(lp[0:1] * lp[1:2], axis=-1, keepdims=True))
           - jnp.exp(jnp.sum(lp[2:3] * lp[3:4], axis=-1, keepdims=True)) + lam_init)
    q = q_ref[0]
    lane = lax.broadcasted_iota(I32, q.shape, 1)
    zero = jnp.zeros_like(q)
    q2 = jnp.concatenate([jnp.where(lane < QK_DIM, q, zero), jnp.where(lane >= QK_DIM, q, zero)], axis=0)

    n_tiles = 2 * tq // MXU_N
    groups = MXU_N // LANES
    m = [jnp.full((1, LANES), -jnp.inf, F32) for _ in range(n_tiles * groups)]
    lsum = [jnp.zeros((SUBLANES, LANES), F32) for _ in range(n_tiles * groups)]
    acc = [jnp.zeros((HEAD_W, MXU_N), F32) for _ in range(n_tiles)]

    def scores(c):
        kc = k_ref[0, c * KV_CHUNK:(c + 1) * KV_CHUNK, :]
        return [lax.dot_general(kc, q2[t * MXU_N:(t + 1) * MXU_N], _NT, preferred_element_type=F32)
                for t in range(n_tiles)]

    pending = [scores(c) for c in range(min(SCORE_LOOKAHEAD, n_chunks))]
    for c in range(n_chunks):
        st = pending.pop(0)
        if c + SCORE_LOOKAHEAD < n_chunks:
            pending.append(scores(c + SCORE_LOOKAHEAD))
        vtc = vt_ref[0, :, c * KV_CHUNK:(c + 1) * KV_CHUNK]
        for t in range(n_tiles):
            ps, alphas = [], []
            for g in range(groups):
                j = t * groups + g
                s = st[t][:, g * LANES:(g + 1) * LANES]
                m_new = jnp.maximum(m[j], jnp.max(s, axis=0, keepdims=True))
                alpha = jnp.exp2(m[j] - m_new)
                p = jnp.exp2(s - m_new)
                lsum[j] = alpha * lsum[j] + jnp.sum(p.reshape(KV_CHUNK // SUBLANES, SUBLANES, LANES), axis=0)
                m[j] = m_new
                ps.append(p.astype(BF16))
                alphas.append(alpha)
            acc[t] = (jnp.concatenate(alphas, axis=1) * acc[t]
                      + jnp.dot(vtc, jnp.concatenate(ps, axis=1), preferred_element_type=F32))
    acc = jnp.concatenate(acc, axis=1)
    lsum = jnp.concatenate(lsum, axis=1)
    o = acc / jnp.sum(lsum, axis=0, keepdims=True)
    od = o[:, :tq] - lam * o[:, tq:]
    y = od * lax.rsqrt(jnp.mean(od * od, axis=0, keepdims=True) + EPS) * sg_ref[...] * (1.0 - lam_init)
    o_ref[0] = y.T.astype(BF16)


def _q_tile(n):
    return Q_TILE_LONG if n >= LONG_SEQ else Q_TILE_SHORT


def _attention(q, k, vt, lam_params, subln_g_col, lam_init):
    b, n, _ = q.shape
    kernel = functools.partial(_attn_kernel, n_chunks=n // KV_CHUNK, lam_init=lam_init)
    tq = _q_tile(n)
    q_spec = pl.BlockSpec((1, tq, HEAD_W), lambda bi, h, i: (bi, i, h))
    return pl.pallas_call(
        kernel,
        grid=(b, HEADS, n // tq),
        in_specs=[
            pl.BlockSpec(lam_params.shape, lambda bi, h, i: (0, 0)),
            q_spec,
            pl.BlockSpec((1, n, HEAD_W), lambda bi, h, i: (bi, 0, h)),
            pl.BlockSpec((1, HEAD_W, n), lambda bi, h, i: (bi, h, 0)),
            pl.BlockSpec((HEAD_W, 1), lambda bi, h, i: (0, 0)),
        ],
        out_specs=q_spec,
        out_shape=jax.ShapeDtypeStruct((b, n, ATTN_WIDTH), BF16),
        compiler_params=pltpu.CompilerParams(
            dimension_semantics=("arbitrary", "arbitrary", "arbitrary"), vmem_limit_bytes=VMEM_LIMIT),
        name="diff_attention",
    )(lam_params, q, k, vt, subln_g_col)


def _first_argmax(vals, best, rows):
    return jnp.min(jnp.where(vals == best, rows, SUBLANES), axis=0, keepdims=True)


def _mix_kernel(x_ref, zp_ref, zprev_ref, znext_ref, icnt_ref, at_ref, wout_ref, wpool_ref, ps_ref, gf_ref,
                wrh_ref, wrl_ref, rb_ref, h_ref, cls_ref, gate_ref, *, seq):
    tt = x_ref.shape[0]
    pos0 = (pl.program_id(0) * tt) % seq
    halo = SUBLANES
    zprev = jnp.where(pos0 == 0, 0.0, zprev_ref[...])
    znext = jnp.where(pos0 + tt == seq, 0.0, znext_ref[...])
    zext = jnp.concatenate([zprev, zp_ref[...], znext], axis=0)
    sub = tt // MIX_SPLIT

    def pool_phase(r0):
        pooled = []
        for g, w in enumerate(POOL_WINDOWS):
            hf = w // 2
            zg = zext[r0:r0 + sub + 2 * halo, g * POOL_GROUP_DIM:(g + 1) * POOL_GROUP_DIM]
            levels = hf.bit_length()
            arr = jnp.concatenate([zg, jnp.zeros((levels * SUBLANES, POOL_GROUP_DIM), F32)], axis=0)
            span = 1
            for _ in range(levels):
                n_out = arr.shape[0] - SUBLANES
                arr = arr[0:n_out] + arr[span:span + n_out]
                span *= 2
            win = arr[halo - hf:halo - hf + sub]
            inv_cnt = icnt_ref[r0:r0 + sub, g * POOL_GROUP_DIM:(g + 1) * POOL_GROUP_DIM]
            pl_g = (win * inv_cnt - zg[halo:halo + sub]).astype(BF16)
            po = jnp.dot(pl_g, wpool_ref[g], preferred_element_type=F32)
            pooled.append((po * ps_ref[:, g * POOL_GROUP_DIM:(g + 1) * POOL_GROUP_DIM]).astype(BF16))
        return jnp.concatenate(pooled + [at_ref[r0:r0 + sub, :]], axis=1)

    def out_phase(r0, mix):
        h = x_ref[r0:r0 + sub, :] + jnp.dot(mix, wout_ref[...], preferred_element_type=F32)
        h_ref[r0:r0 + sub, :] = h
        return h

    def router_phase(h):
        hn = _rms(h, gf_ref[...])
        hi = hn.astype(BF16)
        lo = (hn - hi.astype(F32)).astype(BF16)
        wrh, wrl = wrh_ref[...], wrl_ref[...]
        lt = (lax.dot_general(wrh, hi, _NT, preferred_element_type=F32)
              + lax.dot_general(wrl, hi, _NT, preferred_element_type=F32)
              + lax.dot_general(wrh, lo, _NT, preferred_element_type=F32)) + rb_ref[...]
        rows = lax.broadcasted_iota(I32, (SUBLANES, sub), 0)
        lg = jnp.where(rows < MOE_GROUPS, lt[0:SUBLANES], -jnp.inf)
        eg = jnp.exp(lg - jnp.max(lg, axis=0, keepdims=True))
        pg = eg / jnp.sum(eg, axis=0, keepdims=True)
        g_top = jnp.max(pg, axis=0, keepdims=True)
        g_idx = _first_argmax(pg, g_top, rows)
        le = lt[SUBLANES:2 * SUBLANES]
        for g in range(1, MOE_GROUPS):
            le = jnp.where(g_idx == g, lt[SUBLANES * (g + 1):SUBLANES * (g + 2)], le)
        ee = jnp.exp(le - jnp.max(le, axis=0, keepdims=True))
        pe = ee / jnp.sum(ee, axis=0, keepdims=True)
        e1 = jnp.max(pe, axis=0, keepdims=True)
        i1 = _first_argmax(pe, e1, rows)
        pe2 = jnp.where(rows == i1, -1.0, pe)
        e2 = jnp.max(pe2, axis=0, keepdims=True)
        i2 = _first_argmax(pe2, e2, rows)
        den = e1 + e2
        gate1 = g_top * (e1 / den)
        gate2 = g_top * (e2 / den)
        a = jnp.minimum(i1, i2)
        b = jnp.maximum(i1, i2)
        pair = ((a * (2 * EXPERTS_PER_GROUP - 1 - a)) >> 1) + (b - a - 1)
        first_lower = i1 < i2
        gates = jnp.concatenate(
            [jnp.where(first_lower, gate1, gate2), jnp.where(first_lower, gate2, gate1)], axis=0)
        return g_idx * PAIRS_PER_GROUP + pair, gates

    starts = [i * sub for i in range(MIX_SPLIT)]
    mixes = [pool_phase(r0) for r0 in starts]
    hs = [out_phase(r0, m) for r0, m in zip(starts, mixes)]
    routed = [router_phase(h) for h in hs]
    cls_ref[0] = jnp.concatenate([c for c, _ in routed], axis=1)
    gate_ref[0] = jnp.concatenate([g for _, g in routed], axis=1)


def _pool_inverse_counts(seq):
    pos = jnp.arange(seq, dtype=I32)[:, None]
    half = jnp.repeat(jnp.array(POOL_WINDOWS, I32) // 2, POOL_GROUP_DIM)[None, :]
    cnt = jnp.minimum(pos + half, seq) - jnp.maximum(pos - half, 0)
    return 1.0 / cnt.astype(F32)


def _mix(x2, zp, attn, w_out, w_pool, pool_scale, g_ffn, wr_hi, wr_lo, r_bias, seq):
    t, d = x2.shape
    tt = TOKEN_TILE
    nt = t // tt
    per_seq = seq // tt
    inv_cnt = _pool_inverse_counts(seq)
    hb = tt // SUBLANES
    last_halo = t // SUBLANES - 1
    row = lambda i: (i, 0)
    const2 = lambda i: (0, 0)
    kernel = functools.partial(_mix_kernel, seq=seq)
    return pl.pallas_call(
        kernel,
        grid=(nt,),
        in_specs=[
            pl.BlockSpec((tt, d), row),
            pl.BlockSpec((tt, POOL_WIDTH), row),
            pl.BlockSpec((SUBLANES, POOL_WIDTH), lambda i: (jnp.maximum(i * hb - 1, 0), 0)),
            pl.BlockSpec((SUBLANES, POOL_WIDTH), lambda i: (jnp.minimum((i + 1) * hb, last_halo), 0)),
            pl.BlockSpec((tt, POOL_WIDTH), lambda i: (i % per_seq, 0)),
            pl.BlockSpec((tt, ATTN_WIDTH), row),
            pl.BlockSpec(w_out.shape, const2),
            pl.BlockSpec(w_pool.shape, lambda i: (0, 0, 0)),
            pl.BlockSpec((1, POOL_WIDTH), const2),
            pl.BlockSpec((1, d), const2),
            pl.BlockSpec(wr_hi.shape, const2),
            pl.BlockSpec(wr_lo.shape, const2),
            pl.BlockSpec(r_bias.shape, const2),
        ],
        out_specs=[
            pl.BlockSpec((tt, d), row),
            pl.BlockSpec((1, 1, tt), lambda i: (i, 0, 0)),
            pl.BlockSpec((1, 2, tt), lambda i: (i, 0, 0)),
        ],
        out_shape=[
            jax.ShapeDtypeStruct((t, d), F32),
            jax.ShapeDtypeStruct((nt, 1, tt), I32),
            jax.ShapeDtypeStruct((nt, 2, tt), F32),
        ],
        compiler_params=pltpu.CompilerParams(
            dimension_semantics=("arbitrary",), vmem_limit_bytes=VMEM_LIMIT),
        name="mix_router",
    )(x2, zp, zp, zp, inv_cnt, attn, w_out, w_pool, pool_scale, g_ffn, wr_hi, wr_lo, r_bias)


def _moe_kernel(vblk_ref, vea_ref, veb_ref, vlo_ref, vhi_ref, vfl_ref,
                tokc_ref, tokn_ref, tokp_ref, gates_ref, gf_ref, gfin_ref,
                wga_ref, wua_ref, wda_ref, wgb_ref, wub_ref, wdb_ref, h_hbm,
                out_hbm, hbuf, xnbuf, acc, obuf, gsem, ssem):
    v = pl.program_id(0)
    flags = vfl_ref[v]
    rb = hbuf.shape[1]
    blk = vblk_ref[v]
    slot = blk % 2
    other = 1 - slot
    valid = (flags & V_VALID) != 0
    first = (flags & V_FIRST) != 0
    last = (flags & V_LAST) != 0

    def row_in(tok_ref, r, s):
        return pltpu.make_async_copy(h_hbm.at[pl.ds(tok_ref[0, 0, r], 1), :],
                                     hbuf.at[s, pl.ds(r, 1), :], gsem.at[s])

    def row_out(tok_ref, r, s):
        return pltpu.make_async_copy(obuf.at[s, pl.ds(r, 1), :],
                                     out_hbm.at[pl.ds(tok_ref[0, 0, r], 1), :], ssem.at[s])

    def rolled(fn):
        def body(r, c):
            fn(r)
            return c
        lax.fori_loop(0, rb, body, 0, unroll=8)

    def wait_gather(s):
        rolled(lambda r: row_in(tokc_ref, r, s).wait())

    def wait_scatter(s):
        rolled(lambda r: row_out(tokc_ref, r, s).wait())

    def experts(xn):
        rows = lax.broadcasted_iota(I32, (rb, 1), 0)
        seg = (rows >= vlo_ref[v]) & (rows < vhi_ref[v])
        gts = gates_ref[...]

        def hidden(wg_ref, wu_ref, gate):
            hid = (jax.nn.silu(jnp.dot(xn, wg_ref[0], preferred_element_type=F32))
                   * jnp.dot(xn, wu_ref[0], preferred_element_type=F32))
            return jnp.where(seg, hid * gate, 0.0).astype(BF16)

        ha = hidden(wga_ref, wua_ref, gts[:, 0:1])
        hb = hidden(wgb_ref, wub_ref, gts[:, 1:2])
        return (jnp.dot(ha, wda_ref[0], preferred_element_type=F32)
                + jnp.dot(hb, wdb_ref[0], preferred_element_type=F32))

    def finalize():
        y = _rms(hbuf[slot] + acc[...], gfin_ref[...])

        @pl.when(blk >= 2)
        def _():
            wait_scatter(slot)
        obuf[slot] = y

    @pl.when(valid & first & (v > 0))
    def _first_visit():
        wait_gather(slot)
        for r in range(rb):
            row_in(tokn_ref, r, other).start(priority=r % 2)
        for r in range(rb):
            row_out(tokp_ref, r, other).start(priority=r % 2)
        xn = _rms(hbuf[slot], gf_ref[...]).astype(BF16)
        xnbuf[...] = xn
        acc[...] = experts(xn)

        @pl.when(last)
        def _():
            finalize()

    @pl.when(valid & first & (v == 0))
    def _very_first_visit():
        rolled(lambda r: row_in(tokc_ref, r, slot).start())
        wait_gather(slot)
        rolled(lambda r: row_in(tokn_ref, r, other).start())
        xn = _rms(hbuf[slot], gf_ref[...]).astype(BF16)
        xnbuf[...] = xn
        acc[...] = experts(xn)

        @pl.when(last)
        def _():
            finalize()

    @pl.when(valid & jnp.logical_not(first))
    def _continued_visit():
        acc[...] += experts(xnbuf[...])

        @pl.when(last)
        def _():
            finalize()

    @pl.when((flags & V_FIN) != 0)
    def _final_visit():
        rolled(lambda r: row_out(tokc_ref, r, slot).start())
        wait_scatter(slot)
        wait_scatter(other)
        wait_gather(other)


def _moe(plan, h, gates_rows, g_ffn, g_final, w_gate, w_up, w_down):
    t, d = h.shape
    rb = ROW_BLOCK
    n_blk = t // rb
    n_vis = plan["blk"].shape[0]
    dff = w_gate.shape[2]
    cur = lambda v, blk, *_: (blk[v], 0)
    tok_cur = lambda v, blk, *_: (blk[v], 0, 0)
    tok_nxt = lambda v, blk, *_: (jnp.minimum(blk[v] + 1, n_blk - 1), 0, 0)
    tok_prv = lambda v, blk, *_: (jnp.maximum(blk[v] - 1, 0), 0, 0)
    const2 = lambda v, *_: (0, 0)
    wa = lambda v, blk, ea, *_: (ea[v], 0, 0)
    wb = lambda v, blk, ea, eb, *_: (eb[v], 0, 0)
    grid_spec = pltpu.PrefetchScalarGridSpec(
        num_scalar_prefetch=6,
        grid=(n_vis,),
        in_specs=[
            pl.BlockSpec((1, 1, rb), tok_cur, memory_space=pltpu.SMEM),
            pl.BlockSpec((1, 1, rb), tok_nxt, memory_space=pltpu.SMEM),
            pl.BlockSpec((1, 1, rb), tok_prv, memory_space=pltpu.SMEM),
            pl.BlockSpec((rb, 2), cur),
            pl.BlockSpec((1, d), const2),
            pl.BlockSpec((1, d), const2),
            pl.BlockSpec((1, d, dff), wa),
            pl.BlockSpec((1, d, dff), wa),
            pl.BlockSpec((1, dff, d), wa),
            pl.BlockSpec((1, d, dff), wb),
            pl.BlockSpec((1, d, dff), wb),
            pl.BlockSpec((1, dff, d), wb),
            pl.BlockSpec(memory_space=pl.ANY),
        ],
        out_specs=pl.BlockSpec(memory_space=pl.ANY),
        scratch_shapes=[
            pltpu.VMEM((2, rb, d), F32),
            pltpu.VMEM((rb, d), BF16),
            pltpu.VMEM((rb, d), F32),
            pltpu.VMEM((2, rb, d), F32),
            pltpu.SemaphoreType.DMA((2,)),
            pltpu.SemaphoreType.DMA((2,)),
        ],
    )
    return pl.pallas_call(
        _moe_kernel,
        grid_spec=grid_spec,
        out_shape=jax.ShapeDtypeStruct((t, d), F32),
        compiler_params=pltpu.CompilerParams(
            dimension_semantics=("arbitrary",), vmem_limit_bytes=VMEM_LIMIT),
        name="moe",
    )(plan["blk"], plan["ea"], plan["eb"], plan["lo"], plan["hi"], plan["flags"],
      plan["row_tok"], plan["row_tok"], plan["row_tok"], gates_rows, g_ffn, g_final,
      w_gate, w_up, w_down, w_gate, w_up, w_down, h)


def _pair_tables():
    lo, hi = [], []
    for a in range(EXPERTS_PER_GROUP):
        for b in range(a + 1, EXPERTS_PER_GROUP):
            lo.append(a)
            hi.append(b)
    return jnp.array(lo, I32), jnp.array(hi, I32)


def _dispatch_plan(cls, t):
    rb = ROW_BLOCK
    n_blk = t // rb
    n_vis = n_blk + N_CLASSES - 1
    order = jnp.argsort(cls, stable=True).astype(I32)
    counts = jnp.bincount(cls, length=N_CLASSES).astype(I32)
    cend = jnp.cumsum(counts)
    cstart = cend - counts
    first_blk = cstart // rb
    last_blk = jnp.maximum(cend - 1, 0) // rb
    nvis = jnp.where(counts > 0, last_blk - first_blk + 1, 0)
    vend = jnp.cumsum(nvis)
    vstart = vend - nvis
    total = vend[-1]
    vid = jnp.arange(n_vis, dtype=I32)
    vc = jnp.minimum(vid, total - 1)
    c = jnp.sum((vend[None, :] <= vc[:, None]).astype(I32), axis=1)
    blk = first_blk[c] + (vc - vstart[c])
    lo = jnp.maximum(cstart[c], blk * rb) - blk * rb
    hi = jnp.minimum(cend[c], (blk + 1) * rb) - blk * rb
    valid = vid < total
    prev_blk = jnp.concatenate([jnp.full((1,), -1, I32), blk[:-1]])
    next_blk = jnp.concatenate([blk[1:], jnp.full((1,), -1, I32)])
    first = valid & (blk != prev_blk)
    last = valid & ((blk != next_blk) | (vid == total - 1))
    fin = vid == total - 1
    flags = (valid * V_VALID + first * V_FIRST + last * V_LAST + fin * V_FIN).astype(I32)
    pair_lo, pair_hi = _pair_tables()
    grp = c // PAIRS_PER_GROUP
    pidx = c % PAIRS_PER_GROUP
    plan = dict(blk=blk.astype(I32), ea=(grp * EXPERTS_PER_GROUP + pair_lo[pidx]).astype(I32),
                eb=(grp * EXPERTS_PER_GROUP + pair_hi[pidx]).astype(I32),
                lo=lo.astype(I32), hi=hi.astype(I32), flags=flags,
                row_tok=order.reshape(n_blk, 1, rb))
    return plan, order


def _rope_tables(n):
    inv_freq = ROPE_THETA ** (-jnp.arange(0, QK_DIM, 2, dtype=F32) / QK_DIM)
    ang = jnp.arange(n, dtype=F32)[:, None] * inv_freq[None, :]
    cos, sin = jnp.cos(ang), jnp.sin(ang)
    cos_t = jnp.tile(cos, (1, 2 * HEAD_W // QK_DIM))
    sin_t = jnp.tile(jnp.concatenate([-sin, sin], axis=1), (1, HEAD_W // QK_DIM))
    return cos_t, sin_t


def _trunk(x, p, lam_init):
    b, n, d = x.shape
    t = b * n
    assert n % TOKEN_TILE == 0 and n % _q_tile(n) == 0 and n % KV_CHUNK == 0
    assert t % ROW_BLOCK == 0 and t // ROW_BLOCK >= 2
    x2 = x.reshape(t, d)
    cos, sin = _rope_tables(n)
    zp, q, k, vt = _inproj(x2, p["g_mix"], p["w_in"], p["w_vt"], cos, sin, n)
    attn = _attention(q.reshape(b, n, ATTN_WIDTH), k.reshape(b, n, ATTN_WIDTH), vt,
                      p["lam_params"], p["subln_g"], lam_init)
    h, cls3, gates3 = _mix(x2, zp, attn.reshape(t, ATTN_WIDTH), p["w_out"], p["w_pool"], p["pool_scale"],
                           p["g_ffn"], p["wr_hi"], p["wr_lo"], p["r_bias"], n)
    cls = cls3.reshape(t)
    gates = jnp.transpose(gates3, (1, 0, 2)).reshape(2, t)
    plan, order = _dispatch_plan(cls, t)
    gates_rows = gates[:, order].T
    y = _moe(plan, h, gates_rows, p["g_ffn"], p["g_final"], p["w_gate"], p["w_up"], p["w_down"])
    return y.reshape(b, n, d)


def kernel(x_prompt, x_sample, g_mix, w_in, w_pool, pool_scale, lambda_q1, lambda_k1, lambda_q2, lambda_k2,
           subln_g, w_out, g_ffn, w_router_group, b_router_group, w_router_expert, b_router_expert,
           w_gate, w_up, w_down, g_final):
    depth = g_mix.shape[0]
    assert depth == 1
    l = 0
    lam_init = 0.8 - 0.6 * 1.0
    d = g_mix.shape[1]
    wr = jnp.zeros((ROUTER_ROWS, d), F32)
    wr = wr.at[0:MOE_GROUPS].set(w_router_group[l].T)
    wr = wr.at[SUBLANES:SUBLANES + MOE_GROUPS * EXPERTS_PER_GROUP].set(w_router_expert[l].T)
    wr_hi = wr.astype(BF16)
    wr_lo = (wr - wr_hi.astype(F32)).astype(BF16)
    r_bias = jnp.zeros((ROUTER_ROWS, 1), F32)
    r_bias = r_bias.at[0:MOE_GROUPS, 0].set(b_router_group[l])
    r_bias = r_bias.at[SUBLANES:SUBLANES + MOE_GROUPS * EXPERTS_PER_GROUP, 0].set(b_router_expert[l])
    p = dict(
        g_mix=g_mix[l][None, :], w_in=w_in[l].astype(BF16),
        w_vt=w_in[l][:, POOL_WIDTH + 2 * ATTN_WIDTH:].T.astype(BF16), w_pool=w_pool[l].astype(BF16),
        pool_scale=pool_scale[l][None, :],
        lam_params=jnp.stack([lambda_q1[l], lambda_k1[l], lambda_q2[l], lambda_k2[l]]),
        subln_g=subln_g[l][:, None], w_out=w_out[l].astype(BF16), g_ffn=g_ffn[l][None, :],
        wr_hi=wr_hi, wr_lo=wr_lo, r_bias=r_bias,
        w_gate=w_gate[l].astype(BF16), w_up=w_up[l].astype(BF16), w_down=w_down[l].astype(BF16),
        g_final=g_final[None, :],
    )
    return (_trunk(x_prompt, p, lam_init), _trunk(x_sample, p, lam_init))
```

```python
import functools

import jax
import jax.numpy as jnp
from jax import lax
from jax.experimental import pallas as pl
from jax.experimental.pallas import tpu as pltpu

F32, BF16, I32 = jnp.float32, jnp.bfloat16, jnp.int32

EPS = 1e-6
ROPE_THETA = 10000.0
LOG2_E = 1.4426950408889634
POOL_WINDOWS = (2, 4, 8, 16)
POOL_GROUP_DIM = 128
POOL_WIDTH = POOL_GROUP_DIM * len(POOL_WINDOWS)
HEADS = 4
QK_DIM = 64
HEAD_W = 2 * QK_DIM
ATTN_WIDTH = HEADS * HEAD_W
MOE_GROUPS = 4
EXPERTS_PER_GROUP = 8
PAIRS_PER_GROUP = EXPERTS_PER_GROUP * (EXPERTS_PER_GROUP - 1) // 2
N_CLASSES = MOE_GROUPS * PAIRS_PER_GROUP
ROUTER_ROWS = 48

SUBLANES = 8
LANES = 128
MXU_N = 256
VMEM_LIMIT = 48 * 1024 * 1024

TOKEN_TILE = 512
MIX_SPLIT = 2
Q_TILE_LONG, Q_TILE_SHORT = 256, 512
LONG_SEQ = 4096
KV_CHUNK = 1024
SCORE_LOOKAHEAD = 2
ROW_BLOCK = 256

V_VALID, V_FIRST, V_LAST, V_FIN = 1, 2, 4, 8

_NT = (((1,), (1,)), ((), ()))


def _rms(x, g):
    return x * lax.rsqrt(jnp.mean(x * x, axis=-1, keepdims=True) + EPS) * g


def _inproj_kernel(x_ref, g_ref, w_ref, wvt_ref, cos_ref, sin_ref, zp_ref, q_ref, k_ref, vt_ref):
    xb = _rms(x_ref[...], g_ref[...]).astype(BF16)
    zp_ref[...] = jnp.dot(xb, w_ref[:, 0:POOL_WIDTH], preferred_element_type=F32)
    q0, k0 = POOL_WIDTH, POOL_WIDTH + ATTN_WIDTH
    vt_ref[0] = lax.dot_general(wvt_ref[...], xb, _NT, preferred_element_type=F32).astype(BF16)
    cos, sin = cos_ref[...], sin_ref[...]
    lane = lax.broadcasted_iota(I32, cos.shape, 1)
    lower = (lane % QK_DIM) < (QK_DIM // 2)
    half = QK_DIM // 2

    def rope(z):
        partner = jnp.where(lower, pltpu.roll(z, HEAD_W - half, 1), pltpu.roll(z, half, 1))
        return z * cos + partner * sin

    zq = jnp.dot(xb, w_ref[:, q0:q0 + ATTN_WIDTH], preferred_element_type=F32)
    zk = jnp.dot(xb, w_ref[:, k0:k0 + ATTN_WIDTH], preferred_element_type=F32)
    scale = QK_DIM ** -0.5 * LOG2_E
    for h in range(HEADS):
        sl = slice(h * HEAD_W, (h + 1) * HEAD_W)
        q_ref[:, sl] = (rope(zq[:, sl]) * scale).astype(BF16)
        k_ref[:, sl] = rope(zk[:, sl]).astype(BF16)


def _inproj(x2, g_mix, w_in, w_vt, cos, sin, seq):
    t, d = x2.shape
    tt = TOKEN_TILE
    per_seq = seq // tt
    row = lambda i: (i, 0)
    return pl.pallas_call(
        _inproj_kernel,
        grid=(t // tt,),
        in_specs=[
            pl.BlockSpec((tt, d), row),
            pl.BlockSpec((1, d), lambda i: (0, 0)),
            pl.BlockSpec(w_in.shape, lambda i: (0, 0)),
            pl.BlockSpec(w_vt.shape, lambda i: (0, 0)),
            pl.BlockSpec((tt, HEAD_W), lambda i: (i % per_seq, 0)),
            pl.BlockSpec((tt, HEAD_W), lambda i: (i % per_seq, 0)),
        ],
        out_specs=[
            pl.BlockSpec((tt, POOL_WIDTH), row),
            pl.BlockSpec((tt, ATTN_WIDTH), row),
            pl.BlockSpec((tt, ATTN_WIDTH), row),
            pl.BlockSpec((1, ATTN_WIDTH, tt), lambda i: (i // per_seq, 0, i % per_seq)),
        ],
        out_shape=[
            jax.ShapeDtypeStruct((t, POOL_WIDTH), F32),
            jax.ShapeDtypeStruct((t, ATTN_WIDTH), BF16),
            jax.ShapeDtypeStruct((t, ATTN_WIDTH), BF16),
            jax.ShapeDtypeStruct((t // seq, ATTN_WIDTH, seq), BF16),
        ],
        compiler_params=pltpu.CompilerParams(
            dimension_semantics=("arbitrary",), vmem_limit_bytes=VMEM_LIMIT),
        name="inproj",
    )(x2, g_mix, w_in, w_vt, cos, sin)


def _attn_kernel(lam_ref, q_ref, k_ref, vt_ref, sg_ref, o_ref, *, n_chunks, lam_init):
    tq = q_ref.shape[1]
    lp = lam_ref[...]
    lam = (jnp.exp(jnp.sum(lp[0:1] * lp[1:2], axis=-1, keepdims=True))
           - jnp.exp(jnp.sum(lp[2:3] * lp[3:4], axis=-1, keepdims=True)) + lam_init)
    q = q_ref[0]
    lane = lax.broadcasted_iota(I32, q.shape, 1)
    zero = jnp.zeros_like(q)
    q2 = jnp.concatenate([jnp.where(lane < QK_DIM, q, zero), jnp.where(lane >= QK_DIM, q, zero)], axis=0)

    n_tiles = 2 * tq // MXU_N
    groups = MXU_N // LANES
    m = [jnp.full((1, LANES), -jnp.inf, F32) for _ in range(n_tiles * groups)]
    lsum = [jnp.zeros((SUBLANES, LANES), F32) for _ in range(n_tiles * groups)]
    acc = [jnp.zeros((HEAD_W, MXU_N), F32) for _ in range(n_tiles)]

    def scores(c):
        kc = k_ref[0, c * KV_CHUNK:(c + 1) * KV_CHUNK, :]
        return [lax.dot_general(kc, q2[t * MXU_N:(t + 1) * MXU_N], _NT, preferred_element_type=F32)
                for t in range(n_tiles)]

    pending = [scores(c) for c in range(min(SCORE_LOOKAHEAD, n_chunks))]
    for c in range(n_chunks):
        st = pending.pop(0)
        if c + SCORE_LOOKAHEAD < n_chunks:
            pending.append(scores(c + SCORE_LOOKAHEAD))
        vtc = vt_ref[0, :, c * KV_CHUNK:(c + 1) * KV_CHUNK]
        for t in range(n_tiles):
            ps, alphas = [], []
            for g in range(groups):
                j = t * groups + g
                s = st[t][:, g * LANES:(g + 1) * LANES]
                m_new = jnp.maximum(m[j], jnp.max(s, axis=0, keepdims=True))
                alpha = jnp.exp2(m[j] - m_new)
                p = jnp.exp2(s - m_new)
                lsum[j] = alpha * lsum[j] + jnp.sum(p.reshape(KV_CHUNK // SUBLANES, SUBLANES, LANES), axis=0)
                m[j] = m_new
                ps.append(p.astype(BF16))
                alphas.append(alpha)
            acc[t] = (jnp.concatenate(alphas, axis=1) * acc[t]
                      + jnp.dot(vtc, jnp.concatenate(ps, axis=1), preferred_element_type=F32))
    acc = jnp.concatenate(acc, axis=1)
    lsum = jnp.concatenate(lsum, axis=1)
    o = acc / jnp.sum(lsum, axis=0, keepdims=True)
    od = o[:, :tq] - lam * o[:, tq:]
    y = od * lax.rsqrt(jnp.mean(od * od, axis=0, keepdims=True) + EPS) * sg_ref[...] * (1.0 - lam_init)
    o_ref[0] = y.T.astype(BF16)


def _q_tile(n):
    return Q_TILE_LONG if n >= LONG_SEQ else Q_TILE_SHORT


def _attention(q, k, vt, lam_params, subln_g_col, lam_init):
    b, n, _ = q.shape
    kernel = functools.partial(_attn_kernel, n_chunks=n // KV_CHUNK, lam_init=lam_init)
    tq = _q_tile(n)
    q_spec = pl.BlockSpec((1, tq, HEAD_W), lambda bi, h, i: (bi, i, h))
    return pl.pallas_call(
        kernel,
        grid=(b, HEADS, n // tq),
        in_specs=[
            pl.BlockSpec(lam_params.shape, lambda bi, h, i: (0, 0)),
            q_spec,
            pl.BlockSpec((1, n, HEAD_W), lambda bi, h, i: (bi, 0, h)),
            pl.BlockSpec((1, HEAD_W, n), lambda bi, h, i: (bi, h, 0)),
            pl.BlockSpec((HEAD_W, 1), lambda bi, h, i: (0, 0)),
        ],
        out_specs=q_spec,
        out_shape=jax.ShapeDtypeStruct((b, n, ATTN_WIDTH), BF16),
        compiler_params=pltpu.CompilerParams(
            dimension_semantics=("arbitrary", "arbitrary", "arbitrary"), vmem_limit_bytes=VMEM_LIMIT),
        name="diff_attention",
    )(lam_params, q, k, vt, subln_g_col)


def _first_argmax(vals, best, rows):
    return jnp.min(jnp.where(vals == best, rows, SUBLANES), axis=0, keepdims=True)


def _mix_kernel(x_ref, zp_ref, zprev_ref, znext_ref, icnt_ref, at_ref, wout_ref, wpool_ref, ps_ref, gf_ref,
                wrh_ref, wrl_ref, rb_ref, h_ref, cls_ref, gate_ref, *, seq):
    tt = x_ref.shape[0]
    pos0 = (pl.program_id(0) * tt) % seq
    halo = SUBLANES
    zprev = jnp.where(pos0 == 0, 0.0, zprev_ref[...])
    znext = jnp.where(pos0 + tt == seq, 0.0, znext_ref[...])
    zext = jnp.concatenate([zprev, zp_ref[...], znext], axis=0)
    sub = tt // MIX_SPLIT

    def pool_phase(r0):
        pooled = []
        for g, w in enumerate(POOL_WINDOWS):
            hf = w // 2
            zg = zext[r0:r0 + sub + 2 * halo, g * POOL_GROUP_DIM:(g + 1) * POOL_GROUP_DIM]
            levels = hf.bit_length()
            arr = jnp.concatenate([zg, jnp.zeros((levels * SUBLANES, POOL_GROUP_DIM), F32)], axis=0)
            span = 1
            for _ in range(levels):
                n_out = arr.shape[0] - SUBLANES
                arr = arr[0:n_out] + arr[span:span + n_out]
                span *= 2
            win = arr[halo - hf:halo - hf + sub]
            inv_cnt = icnt_ref[r0:r0 + sub, g * POOL_GROUP_DIM:(g + 1) * POOL_GROUP_DIM]
            pl_g = (win * inv_cnt - zg[halo:halo + sub]).astype(BF16)
            po = jnp.dot(pl_g, wpool_ref[g], preferred_element_type=F32)
            pooled.append((po * ps_ref[:, g * POOL_GROUP_DIM:(g + 1) * POOL_GROUP_DIM]).astype(BF16))
        return jnp.concatenate(pooled + [at_ref[r0:r0 + sub, :]], axis=1)

    def out_phase(r0, mix):
        h = x_ref[r0:r0 + sub, :] + jnp.dot(mix, wout_ref[...], preferred_element_type=F32)
        h_ref[r0:r0 + sub, :] = h
        return h

    def router_phase(h):
        hn = _rms(h, gf_ref[...])
        hi = hn.astype(BF16)
        lo = (hn - hi.astype(F32)).astype(BF16)
        wrh, wrl = wrh_ref[...], wrl_ref[...]
        lt = (lax.dot_general(wrh, hi, _NT, preferred_element_type=F32)
              + lax.dot_general(wrl, hi, _NT, preferred_element_type=F32)
              + lax.dot_general(wrh, lo, _NT, preferred_element_type=F32)) + rb_ref[...]
        rows = lax.broadcasted_iota(I32, (SUBLANES, sub), 0)
        lg = jnp.where(rows < MOE_GROUPS, lt[0:SUBLANES], -jnp.inf)
        eg = jnp.exp(lg - jnp.max(lg, axis=0, keepdims=True))
        pg = eg / jnp.sum(eg, axis=0, keepdims=True)
        g_top = jnp.max(pg, axis=0, keepdims=True)
        g_idx = _first_argmax(pg, g_top, rows)
        le = lt[SUBLANES:2 * SUBLANES]
        for g in range(1, MOE_GROUPS):
            le = jnp.where(g_idx == g, lt[SUBLANES * (g + 1):SUBLANES * (g + 2)], le)
        ee = jnp.exp(le - jnp.max(le, axis=0, keepdims=True))
        pe = ee / jnp.sum(ee, axis=0, keepdims=True)
        e1 = jnp.max(pe, axis=0, keepdims=True)
        i1 = _first_argmax(pe, e1, rows)
        pe2 = jnp.where(rows == i1, -1.0, pe)
        e2 = jnp.max(pe2, axis=0, keepdims=True)
        i2 = _first_argmax(pe2, e2, rows)
        den = e1 + e2
        gate1 = g_top * (e1 / den)
        gate2 = g_top * (e2 / den)
        a = jnp.minimum(i1, i2)
        b = jnp.maximum(i1, i2)
        pair = ((a * (2 * EXPERTS_PER_GROUP - 1 - a)) >> 1) + (b - a - 1)
        first_lower = i1 < i2
        gates = jnp.concatenate(
            [jnp.where(first_lower, gate1, gate2), jnp.where(first_lower, gate2, gate1)], axis=0)
        return g_idx * PAIRS_PER_GROUP + pair, gates

    starts = [i * sub for i in range(MIX_SPLIT)]
    mixes = [pool_phase(r0) for r0 in starts]
    hs = [out_phase(r0, m) for r0, m in zip(starts, mixes)]
    routed = [router_phase(h) for h in hs]
    cls_ref[0] = jnp.concatenate([c for c, _ in routed], axis=1)
    gate_ref[0] = jnp.concatenate([g for _, g in routed], axis=1)


def _pool_inverse_counts(seq):
    pos = jnp.arange(seq, dtype=I32)[:, None]
    half = jnp.repeat(jnp.array(POOL_WINDOWS, I32) // 2, POOL_GROUP_DIM)[None, :]
    cnt = jnp.minimum(pos + half, seq) - jnp.maximum(pos - half, 0)
    return 1.0 / cnt.astype(F32)


def _mix(x2, zp, attn, w_out, w_pool, pool_scale, g_ffn, wr_hi, wr_lo, r_bias, seq):
    t, d = x2.shape
    tt = TOKEN_TILE
    nt = t // tt
    per_seq = seq // tt
    inv_cnt = _pool_inverse_counts(seq)
    hb = tt // SUBLANES
    last_halo = t // SUBLANES - 1
    row = lambda i: (i, 0)
    const2 = lambda i: (0, 0)
    kernel = functools.partial(_mix_kernel, seq=seq)
    return pl.pallas_call(
        kernel,
        grid=(nt,),
        in_specs=[
            pl.BlockSpec((tt, d), row),
            pl.BlockSpec((tt, POOL_WIDTH), row),
            pl.BlockSpec((SUBLANES, POOL_WIDTH), lambda i: (jnp.maximum(i * hb - 1, 0), 0)),
            pl.BlockSpec((SUBLANES, POOL_WIDTH), lambda i: (jnp.minimum((i + 1) * hb, last_halo), 0)),
            pl.BlockSpec((tt, POOL_WIDTH), lambda i: (i % per_seq, 0)),
            pl.BlockSpec((tt, ATTN_WIDTH), row),
            pl.BlockSpec(w_out.shape, const2),
            pl.BlockSpec(w_pool.shape, lambda i: (0, 0, 0)),
            pl.BlockSpec((1, POOL_WIDTH), const2),
            pl.BlockSpec((1, d), const2),
            pl.BlockSpec(wr_hi.shape, const2),
            pl.BlockSpec(wr_lo.shape, const2),
            pl.BlockSpec(r_bias.shape, const2),
        ],
        out_specs=[
            pl.BlockSpec((tt, d), row),
            pl.BlockSpec((1, 1, tt), lambda i: (i, 0, 0)),
            pl.BlockSpec((1, 2, tt), lambda i: (i, 0, 0)),
        ],
        out_shape=[
            jax.ShapeDtypeStruct((t, d), F32),
            jax.ShapeDtypeStruct((nt, 1, tt), I32),
            jax.ShapeDtypeStruct((nt, 2, tt), F32),
        ],
        compiler_params=pltpu.CompilerParams(
            dimension_semantics=("arbitrary",), vmem_limit_bytes=VMEM_LIMIT),
        name="mix_router",
    )(x2, zp, zp, zp, inv_cnt, attn, w_out, w_pool, pool_scale, g_ffn, wr_hi, wr_lo, r_bias)


def _moe_kernel(vblk_ref, vea_ref, veb_ref, vlo_ref, vhi_ref, vfl_ref,
                tokc_ref, tokn_ref, tokp_ref, gates_ref, gf_ref, gfin_ref,
                wga_ref, wua_ref, wda_ref, wgb_ref, wub_ref, wdb_ref, h_hbm,
                out_hbm, hbuf, xnbuf, acc, obuf, gsem, ssem):
    v = pl.program_id(0)
    flags = vfl_ref[v]
    rb = hbuf.shape[1]
    blk = vblk_ref[v]
    slot = blk % 2
    other = 1 - slot
    valid = (flags & V_VALID) != 0
    first = (flags & V_FIRST) != 0
    last = (flags & V_LAST) != 0

    def row_in(tok_ref, r, s):
        return pltpu.make_async_copy(h_hbm.at[pl.ds(tok_ref[0, 0, r], 1), :],
                                     hbuf.at[s, pl.ds(r, 1), :], gsem.at[s])

    def row_out(tok_ref, r, s):
        return pltpu.make_async_copy(obuf.at[s, pl.ds(r, 1), :],
                                     out_hbm.at[pl.ds(tok_ref[0, 0, r], 1), :], ssem.at[s])

    def rolled(fn):
        def body(r, c):
            fn(r)
            return c
        lax.fori_loop(0, rb, body, 0, unroll=8)

    def wait_gather(s):
        rolled(lambda r: row_in(tokc_ref, r, s).wait())

    def wait_scatter(s):
        rolled(lambda r: row_out(tokc_ref, r, s).wait())

    def experts(xn):
        rows = lax.broadcasted_iota(I32, (rb, 1), 0)
        seg = (rows >= vlo_ref[v]) & (rows < vhi_ref[v])
        gts = gates_ref[...]

        def hidden(wg_ref, wu_ref, gate):
            hid = (jax.nn.silu(jnp.dot(xn, wg_ref[0], preferred_element_type=F32))
                   * jnp.dot(xn, wu_ref[0], preferred_element_type=F32))
            return jnp.where(seg, hid * gate, 0.0).astype(BF16)

        ha = hidden(wga_ref, wua_ref, gts[:, 0:1])
        hb = hidden(wgb_ref, wub_ref, gts[:, 1:2])
        return (jnp.dot(ha, wda_ref[0], preferred_element_type=F32)
                + jnp.dot(hb, wdb_ref[0], preferred_element_type=F32))

    def finalize():
        y = _rms(hbuf[slot] + acc[...], gfin_ref[...])

        @pl.when(blk >= 2)
        def _():
            wait_scatter(slot)
        obuf[slot] = y

    def first_visit(s):
        wait_gather(s)
        for r in range(rb):
            row_in(tokn_ref, r, 1 - s).start(priority=r % 2)
        for r in range(rb):
            row_out(tokp_ref, r, 1 - s).start(priority=r % 2)
        xn = _rms(hbuf[s], gf_ref[...]).astype(BF16)
        xnbuf[...] = xn
        acc[...] = experts(xn)

        @pl.when(last)
        def _():
            finalize()

    for s in range(2):
        pl.when(valid & first & (v > 0) & (slot == s))(functools.partial(first_visit, s))

    @pl.when(valid & first & (v == 0))
    def _very_first_visit():
        rolled(lambda r: row_in(tokc_ref, r, slot).start())
        wait_gather(slot)
        rolled(lambda r: row_in(tokn_ref, r, other).start())
        xn = _rms(hbuf[slot], gf_ref[...]).astype(BF16)
        xnbuf[...] = xn
        acc[...] = experts(xn)

        @pl.when(last)
        def _():
            finalize()

    @pl.when(valid & jnp.logical_not(first))
    def _continued_visit():
        acc[...] += experts(xnbuf[...])

        @pl.when(last)
        def _():
            finalize()

    @pl.when((flags & V_FIN) != 0)
    def _final_visit():
        rolled(lambda r: row_out(tokc_ref, r, slot).start())
        wait_scatter(slot)
        wait_scatter(other)
        wait_gather(other)


def _moe(plan, h, gates_rows, g_ffn, g_final, w_gate, w_up, w_down):
    t, d = h.shape
    rb = ROW_BLOCK
    n_blk = t // rb
    n_vis = plan["blk"].shape[0]
    dff = w_gate.shape[2]
    cur = lambda v, blk, *_: (blk[v], 0)
    tok_cur = lambda v, blk, *_: (blk[v], 0, 0)
    tok_nxt = lambda v, blk, *_: (jnp.minimum(blk[v] + 1, n_blk - 1), 0, 0)
    tok_prv = lambda v, blk, *_: (jnp.maximum(blk[v] - 1, 0), 0, 0)
    const2 = lambda v, *_: (0, 0)
    wa = lambda v, blk, ea, *_: (ea[v], 0, 0)
    wb = lambda v, blk, ea, eb, *_: (eb[v], 0, 0)
    grid_spec = pltpu.PrefetchScalarGridSpec(
        num_scalar_prefetch=6,
        grid=(n_vis,),
        in_specs=[
            pl.BlockSpec((1, 1, rb), tok_cur, memory_space=pltpu.SMEM),
            pl.BlockSpec((1, 1, rb), tok_nxt, memory_space=pltpu.SMEM),
            pl.BlockSpec((1, 1, rb), tok_prv, memory_space=pltpu.SMEM),
            pl.BlockSpec((rb, 2), cur),
            pl.BlockSpec((1, d), const2),
            pl.BlockSpec((1, d), const2),
            pl.BlockSpec((1, d, dff), wa),
            pl.BlockSpec((1, d, dff), wa),
            pl.BlockSpec((1, dff, d), wa),
            pl.BlockSpec((1, d, dff), wb),
            pl.BlockSpec((1, d, dff), wb),
            pl.BlockSpec((1, dff, d), wb),
            pl.BlockSpec(memory_space=pl.ANY),
        ],
        out_specs=pl.BlockSpec(memory_space=pl.ANY),
        scratch_shapes=[
            pltpu.VMEM((2, rb, d), F32),
            pltpu.VMEM((rb, d), BF16),
            pltpu.VMEM((rb, d), F32),
            pltpu.VMEM((2, rb, d), F32),
            pltpu.SemaphoreType.DMA((2,)),
            pltpu.SemaphoreType.DMA((2,)),
        ],
    )
    return pl.pallas_call(
        _moe_kernel,
        grid_spec=grid_spec,
        out_shape=jax.ShapeDtypeStruct((t, d), F32),
        compiler_params=pltpu.CompilerParams(
            dimension_semantics=("arbitrary",), vmem_limit_bytes=VMEM_LIMIT),
        name="moe",
    )(plan["blk"], plan["ea"], plan["eb"], plan["lo"], plan["hi"], plan["flags"],
      plan["row_tok"], plan["row_tok"], plan["row_tok"], gates_rows, g_ffn, g_final,
      w_gate, w_up, w_down, w_gate, w_up, w_down, h)


def _pair_tables():
    lo, hi = [], []
    for a in range(EXPERTS_PER_GROUP):
        for b in range(a + 1, EXPERTS_PER_GROUP):
            lo.append(a)
            hi.append(b)
    return jnp.array(lo, I32), jnp.array(hi, I32)


def _dispatch_plan(cls, t):
    rb = ROW_BLOCK
    n_blk = t // rb
    n_vis = n_blk + N_CLASSES - 1
    order = jnp.argsort(cls, stable=True).astype(I32)
    counts = jnp.bincount(cls, length=N_CLASSES).astype(I32)
    cend = jnp.cumsum(counts)
    cstart = cend - counts
    first_blk = cstart // rb
    last_blk = jnp.maximum(cend - 1, 0) // rb
    nvis = jnp.where(counts > 0, last_blk - first_blk + 1, 0)
    vend = jnp.cumsum(nvis)
    vstart = vend - nvis
    total = vend[-1]
    vid = jnp.arange(n_vis, dtype=I32)
    vc = jnp.minimum(vid, total - 1)
    c = jnp.sum((vend[None, :] <= vc[:, None]).astype(I32), axis=1)
    blk = first_blk[c] + (vc - vstart[c])
    lo = jnp.maximum(cstart[c], blk * rb) - blk * rb
    hi = jnp.minimum(cend[c], (blk + 1) * rb) - blk * rb
    valid = vid < total
    prev_blk = jnp.concatenate([jnp.full((1,), -1, I32), blk[:-1]])
    next_blk = jnp.concatenate([blk[1:], jnp.full((1,), -1, I32)])
    first = valid & (blk != prev_blk)
    last = valid & ((blk != next_blk) | (vid == total - 1))
    fin = vid == total - 1
    flags = (valid * V_VALID + first * V_FIRST + last * V_LAST + fin * V_FIN).astype(I32)
    pair_lo, pair_hi = _pair_tables()
    grp = c // PAIRS_PER_GROUP
    pidx = c % PAIRS_PER_GROUP
    plan = dict(blk=blk.astype(I32), ea=(grp * EXPERTS_PER_GROUP + pair_lo[pidx]).astype(I32),
                eb=(grp * EXPERTS_PER_GROUP + pair_hi[pidx]).astype(I32),
                lo=lo.astype(I32), hi=hi.astype(I32), flags=flags,
                row_tok=order.reshape(n_blk, 1, rb))
    return plan, order


def _rope_tables(n):
    inv_freq = ROPE_THETA ** (-jnp.arange(0, QK_DIM, 2, dtype=F32) / QK_DIM)
    ang = jnp.arange(n, dtype=F32)[:, None] * inv_freq[None, :]
    cos, sin = jnp.cos(ang), jnp.sin(ang)
    cos_t = jnp.tile(cos, (1, 2 * HEAD_W // QK_DIM))
    sin_t = jnp.tile(jnp.concatenate([-sin, sin], axis=1), (1, HEAD_W // QK_DIM))
    return cos_t, sin_t


def _trunk(x, p, lam_init):
    b, n, d = x.shape
    t = b * n
    assert n % TOKEN_TILE == 0 and n % _q_tile(n) == 0 and n % KV_CHUNK == 0
    assert t % ROW_BLOCK == 0 and t // ROW_BLOCK >= 2
    x2 = x.reshape(t, d)
    cos, sin = _rope_tables(n)
    zp, q, k, vt = _inproj(x2, p["g_mix"], p["w_in"], p["w_vt"], cos, sin, n)
    attn = _attention(q.reshape(b, n, ATTN_WIDTH), k.reshape(b, n, ATTN_WIDTH), vt,
                      p["lam_params"], p["subln_g"], lam_init)
    h, cls3, gates3 = _mix(x2, zp, attn.reshape(t, ATTN_WIDTH), p["w_out"], p["w_pool"], p["pool_scale"],
                           p["g_ffn"], p["wr_hi"], p["wr_lo"], p["r_bias"], n)
    cls = cls3.reshape(t)
    gates = jnp.transpose(gates3, (1, 0, 2)).reshape(2, t)
    plan, order = _dispatch_plan(cls, t)
    gates_rows = gates[:, order].T
    y = _moe(plan, h, gates_rows, p["g_ffn"], p["g_final"], p["w_gate"], p["w_up"], p["w_down"])
    return y.reshape(b, n, d)


def kernel(x_prompt, x_sample, g_mix, w_in, w_pool, pool_scale, lambda_q1, lambda_k1, lambda_q2, lambda_k2,
           subln_g, w_out, g_ffn, w_router_group, b_router_group, w_router_expert, b_router_expert,
           w_gate, w_up, w_down, g_final):
    depth = g_mix.shape[0]
    assert depth == 1
    l = 0
    lam_init = 0.8 - 0.6 * 1.0
    d = g_mix.shape[1]
    wr = jnp.zeros((ROUTER_ROWS, d), F32)
    wr = wr.at[0:MOE_GROUPS].set(w_router_group[l].T)
    wr = wr.at[SUBLANES:SUBLANES + MOE_GROUPS * EXPERTS_PER_GROUP].set(w_router_expert[l].T)
    wr_hi = wr.astype(BF16)
    wr_lo = (wr - wr_hi.astype(F32)).astype(BF16)
    r_bias = jnp.zeros((ROUTER_ROWS, 1), F32)
    r_bias = r_bias.at[0:MOE_GROUPS, 0].set(b_router_group[l])
    r_bias = r_bias.at[SUBLANES:SUBLANES + MOE_GROUPS * EXPERTS_PER_GROUP, 0].set(b_router_expert[l])
    p = dict(
        g_mix=g_mix[l][None, :], w_in=w_in[l].astype(BF16),
        w_vt=w_in[l][:, POOL_WIDTH + 2 * ATTN_WIDTH:].T.astype(BF16), w_pool=w_pool[l].astype(BF16),
        pool_scale=pool_scale[l][None, :],
        lam_params=jnp.stack([lambda_q1[l], lambda_k1[l], lambda_q2[l], lambda_k2[l]]),
        subln_g=subln_g[l][:, None], w_out=w_out[l].astype(BF16), g_ffn=g_ffn[l][None, :],
        wr_hi=wr_hi, wr_lo=wr_lo, r_bias=r_bias,
        w_gate=w_gate[l].astype(BF16), w_up=w_up[l].astype(BF16), w_down=w_down[l].astype(BF16),
        g_final=g_final[None, :],
    )
    return (_trunk(x_prompt, p, lam_init), _trunk(x_sample, p, lam_init))
```
